```python
import jax, jax.numpy as jnp
from jax import lax
import numpy as np

D_MODEL = 1024
BATCH = 4
SEQ = 4096
DEPTH = 2

N_MIXERS = 2
N_LRU_LAYERS = (DEPTH + 1) // 2
N_SB_LAYERS = DEPTH // 2
LRU_HEADS = max(4, D_MODEL // 64)
LRU_BLOCK = max(16, (((4 * D_MODEL) // 3) // LRU_HEADS) // 16 * 16)
LRU_WIDTH = LRU_HEADS * LRU_BLOCK
LRU_C = 8.0
CONV_WIDTH = 4
SB_HEAD_DIM = 64
SB_HEADS = max(4, D_MODEL // SB_HEAD_DIM)
SB_WIDTH = SB_HEADS * SB_HEAD_DIM
Q_BLOCK = 128
RMS_EPS = 1e-6

kernel_name = "hybrid_rglru_stickbreaking_trunk"


def rms_norm(x, g):
    xf = x.astype(jnp.float32)
    y = xf * lax.rsqrt(jnp.mean(xf * xf, axis=-1, keepdims=True) + RMS_EPS)
    return (y * g.astype(jnp.float32)).astype(x.dtype)


def causal_depthwise_conv(x, w, b):
    c = x.shape[-1]
    y = lax.conv_general_dilated(
        x, w[:, None, :].astype(x.dtype), window_strides=(1,),
        padding=[(CONV_WIDTH - 1, 0)], dimension_numbers=("NWC", "WIO", "NWC"),
        feature_group_count=c)
    return y + b.astype(x.dtype)


def _linear_recurrence_combine(left, right):
    a1, b1 = left
    a2, b2 = right
    return a1 * a2, a2 * b1 + b2


def rg_lru(x, w_gates, b_gates, lam):
    bsz, s, _ = x.shape
    xh = x.reshape(bsz, s, LRU_HEADS, LRU_BLOCK)
    gl = jnp.einsum("bshi,hio->bsho", xh, w_gates.astype(x.dtype)).astype(jnp.float32) + b_gates.astype(jnp.float32)
    r = jax.nn.sigmoid(gl[..., :LRU_BLOCK]).reshape(bsz, s, LRU_WIDTH)
    i = jax.nn.sigmoid(gl[..., LRU_BLOCK:]).reshape(bsz, s, LRU_WIDTH)
    log_a = LRU_C * r * jax.nn.log_sigmoid(lam.astype(jnp.float32))
    a = jnp.exp(log_a)
    mult = jnp.sqrt(-jnp.expm1(2.0 * log_a))
    bterm = mult * (i * x.astype(jnp.float32))
    _, h = lax.associative_scan(_linear_recurrence_combine, (a, bterm), axis=1)
    return h.astype(x.dtype)


def lru_mixer(h, w_in, conv_w, conv_b, w_gates, b_gates, lam, w_out):
    u = h @ w_in.astype(h.dtype)
    xb, gate = u[..., :LRU_WIDTH], u[..., LRU_WIDTH:]
    xb = causal_depthwise_conv(xb, conv_w, conv_b)
    y = rg_lru(xb, w_gates, b_gates, lam)
    return (y * jax.nn.silu(gate)) @ w_out.astype(h.dtype)


def stick_breaking_attention(q, k, v):
    s_len = q.shape[1]
    scale = SB_HEAD_DIM ** -0.5
    outs = []
    for qb in range(s_len // Q_BLOCK):
        t0 = qb * Q_BLOCK
        n_k = t0 + Q_BLOCK
        z = jnp.einsum("bthd,bshd->bhts", q[:, t0:n_k], k[:, :n_k]).astype(jnp.float32) * scale
        t_idx = t0 + jnp.arange(Q_BLOCK)[:, None]
        s_idx = jnp.arange(n_k)[None, :]
        causal = s_idx < t_idx
        log_beta = jax.nn.log_sigmoid(z)
        log_fail = jnp.where(causal, jax.nn.log_sigmoid(-z), 0.0)
        later = lax.cumsum(log_fail, axis=3, reverse=True) - log_fail
        att = jnp.where(causal, jnp.exp(log_beta + later), 0.0)
        outs.append(jnp.einsum("bhts,bshd->bthd", att.astype(v.dtype), v[:, :n_k]))
    return jnp.concatenate(outs, axis=1)


def sb_mixer(h, w_in, w_out):
    bsz, s, _ = h.shape
    u = h @ w_in.astype(h.dtype)
    q = u[..., :SB_WIDTH].reshape(bsz, s, SB_HEADS, SB_HEAD_DIM)
    k = u[..., SB_WIDTH:2 * SB_WIDTH].reshape(bsz, s, SB_HEADS, SB_HEAD_DIM)
    v = u[..., 2 * SB_WIDTH:3 * SB_WIDTH].reshape(bsz, s, SB_HEADS, SB_HEAD_DIM)
    gate = u[..., 3 * SB_WIDTH:]
    o = stick_breaking_attention(q, k, v).reshape(bsz, s, SB_WIDTH)
    return (o * jax.nn.silu(gate)) @ w_out.astype(h.dtype)


def setup_inputs(seed: int = 0) -> dict:
    key = jax.random.key(seed)
    ks = jax.random.split(key, 12)
    f32 = jnp.float32
    x = jax.random.normal(ks[0], (BATCH, SEQ, D_MODEL), f32)
    norm_g = 1.0 + 0.02 * jax.random.normal(ks[1], (DEPTH, D_MODEL), f32)
    final_norm_g = 1.0 + 0.02 * jax.random.normal(ks[2], (D_MODEL,), f32)
    lru_w_in = jax.random.normal(ks[3], (N_LRU_LAYERS, D_MODEL, 2 * LRU_WIDTH), f32) * D_MODEL ** -0.5
    lru_conv_w = jax.random.normal(ks[4], (N_LRU_LAYERS, CONV_WIDTH, LRU_WIDTH), f32) * CONV_WIDTH ** -0.5
    lru_conv_b = 0.01 * jax.random.normal(ks[5], (N_LRU_LAYERS, LRU_WIDTH), f32)
    lru_w_gates = jax.random.normal(ks[6], (N_LRU_LAYERS, LRU_HEADS, LRU_BLOCK, 2 * LRU_BLOCK), f32) * LRU_BLOCK ** -0.5
    lru_b_gates = 0.01 * jax.random.normal(ks[7], (N_LRU_LAYERS, LRU_HEADS, 2 * LRU_BLOCK), f32)
    u = jax.random.uniform(ks[8], (N_LRU_LAYERS, LRU_WIDTH), f32, minval=0.9, maxval=0.999)
    a0 = u ** (1.0 / LRU_C)
    lru_lambda = jnp.log(a0) - jnp.log1p(-a0)
    lru_w_out = jax.random.normal(ks[9], (N_LRU_LAYERS, LRU_WIDTH, D_MODEL), f32) * LRU_WIDTH ** -0.5
    sb_w_in = jax.random.normal(ks[10], (N_SB_LAYERS, D_MODEL, 4 * SB_WIDTH), f32) * D_MODEL ** -0.5
    sb_w_out = jax.random.normal(ks[11], (N_SB_LAYERS, SB_WIDTH, D_MODEL), f32) * SB_WIDTH ** -0.5
    return {"x": x, "norm_g": norm_g, "final_norm_g": final_norm_g,
            "lru_w_in": lru_w_in, "lru_conv_w": lru_conv_w, "lru_conv_b": lru_conv_b,
            "lru_w_gates": lru_w_gates, "lru_b_gates": lru_b_gates, "lru_lambda": lru_lambda,
            "lru_w_out": lru_w_out, "sb_w_in": sb_w_in, "sb_w_out": sb_w_out}


def reference(x, norm_g, final_norm_g, lru_w_in, lru_conv_w, lru_conv_b, lru_w_gates,
              lru_b_gates, lru_lambda, lru_w_out, sb_w_in, sb_w_out):
    for layer in range(DEPTH):
        h = rms_norm(x, norm_g[layer])
        j = layer // N_MIXERS
        if layer % N_MIXERS == 0:
            y = lru_mixer(h, lru_w_in[j], lru_conv_w[j], lru_conv_b[j], lru_w_gates[j],
                          lru_b_gates[j], lru_lambda[j], lru_w_out[j])
        else:
            y = sb_mixer(h, sb_w_in[j], sb_w_out[j])
        x = x + y
    return rms_norm(x, final_norm_g)
```

```python
import functools

import jax
import jax.numpy as jnp
from jax import lax
from jax.experimental import pallas as pl
from jax.experimental.pallas import tpu as pltpu

F32 = jnp.float32
BF16 = jnp.bfloat16

D_MODEL = 1024
LRU_HEADS = 16
LRU_BLOCK = 80
LRU_WIDTH = LRU_HEADS * LRU_BLOCK
LRU_C = 8.0
CONV_WIDTH = 4
SB_HEAD_DIM = 64
SB_HEADS = 16
SB_WIDTH = SB_HEADS * SB_HEAD_DIM
RMS_EPS = 1e-6

SUBLANES = 8
LRU_TS = 256
SB_T = 256
HEAD_PAIR = 2 * SB_HEAD_DIM
VMEM_LIMIT = 56 * 1024 * 1024


def _rms_norm(x, g):
    ms = jnp.mean(x * x, axis=-1, keepdims=True)
    return x * lax.rsqrt(ms + RMS_EPS) * g


def _softplus(z):
    return jnp.maximum(z, 0.0) + jnp.log(1.0 + jnp.exp(-jnp.abs(z)))


def _lru_layer_kernel(x_ref, g_ref, win_ref, cw_ref, cb_ref, wg_ref, bg_ref, lam_ref, wout_ref,
                      o_ref, xpad_ref, hprev_ref, a_ref, b_ref):
    ts = LRU_TS
    c = LRU_WIDTH

    @pl.when(pl.program_id(1) == 0)
    def _():
        xpad_ref[0:SUBLANES, :] = jnp.zeros((SUBLANES, c), F32)
        hprev_ref[...] = jnp.zeros((SUBLANES, c), F32)

    xt = x_ref[0]
    h = _rms_norm(xt, g_ref[...]).astype(BF16)
    u = jnp.dot(h, win_ref[...], preferred_element_type=F32)
    gate = u[:, c:]

    xpad_ref[SUBLANES:SUBLANES + ts, :] = u[:, :c]
    cw = cw_ref[...]
    xc = cb_ref[...]
    for tap in range(CONV_WIDTH):
        off = SUBLANES - (CONV_WIDTH - 1) + tap
        xc = xc + cw[tap:tap + 1, :] * xpad_ref[off:off + ts, :]
    xpad_ref[0:SUBLANES, :] = xpad_ref[ts:ts + SUBLANES, :]

    gl = jnp.dot(xc.astype(BF16), wg_ref[...], preferred_element_type=F32) + bg_ref[...]
    r = jax.nn.sigmoid(gl[:, :c])
    i = jax.nn.sigmoid(gl[:, c:])
    lam = lam_ref[...]
    log_sig_lam = jnp.minimum(lam, 0.0) - jnp.log(1.0 + jnp.exp(-jnp.abs(lam)))
    log_a = (LRU_C * r) * log_sig_lam
    a = jnp.exp(log_a)
    a_ref[...] = a
    b_ref[...] = jnp.sqrt(jnp.tanh(-log_a) * (1.0 + a * a)) * (i * xc)

    row = lax.broadcasted_iota(jnp.int32, (SUBLANES, c), 0)

    def scan_group(gidx, hprev):
        off = pl.multiple_of(gidx * SUBLANES, SUBLANES)
        av = a_ref[pl.ds(off, SUBLANES), :]
        bv = b_ref[pl.ds(off, SUBLANES), :]
        for k in (1, 2, 4):
            a_sh = jnp.where(row >= k, pltpu.roll(av, k, 0), 1.0)
            b_sh = jnp.where(row >= k, pltpu.roll(bv, k, 0), 0.0)
            bv = av * b_sh + bv
            av = av * a_sh
        hv = av * hprev + bv
        b_ref[pl.ds(off, SUBLANES), :] = hv
        return jnp.broadcast_to(hv[SUBLANES - 1:SUBLANES, :], (SUBLANES, c))

    hprev_ref[...] = lax.fori_loop(0, ts // SUBLANES, scan_group, hprev_ref[...])

    y = (b_ref[...] * (gate * jax.nn.sigmoid(gate))).astype(BF16)
    o_ref[0] = xt + jnp.dot(y, wout_ref[...], preferred_element_type=F32)


def _lru_layer(x, g, w_in, conv_w, conv_b, w_gates, b_gates, lam, w_out):
    bsz, s, d = x.shape
    c = LRU_WIDTH
    const = lambda b, t: (0, 0)
    return pl.pallas_call(
        _lru_layer_kernel,
        grid=(bsz, s // LRU_TS),
        in_specs=[
            pl.BlockSpec((1, LRU_TS, d), lambda b, t: (b, t, 0)),
            pl.BlockSpec((1, d), const),
            pl.BlockSpec((d, 2 * c), const),
            pl.BlockSpec((CONV_WIDTH, c), const),
            pl.BlockSpec((1, c), const),
            pl.BlockSpec((c, 2 * c), const),
            pl.BlockSpec((1, 2 * c), const),
            pl.BlockSpec((1, c), const),
            pl.BlockSpec((c, d), const),
        ],
        out_specs=pl.BlockSpec((1, LRU_TS, d), lambda b, t: (b, t, 0)),
        out_shape=jax.ShapeDtypeStruct((bsz, s, d), F32),
        scratch_shapes=[
            pltpu.VMEM((LRU_TS + SUBLANES, c), F32),
            pltpu.VMEM((SUBLANES, c), F32),
            pltpu.VMEM((LRU_TS, c), F32),
            pltpu.VMEM((LRU_TS, c), F32),
        ],
        compiler_params=pltpu.CompilerParams(
            dimension_semantics=("arbitrary", "arbitrary"), vmem_limit_bytes=VMEM_LIMIT),
        name="lru_layer",
    )(x, g, w_in, conv_w, conv_b, w_gates, b_gates, lam, w_out)


def _sb_inproj_kernel(x_ref, g_ref, wk_ref, w3t_ref, k_ref, qt_ref, vt_ref, gt_ref):
    h = _rms_norm(x_ref[0], g_ref[...]).astype(BF16)
    k_ref[0] = jnp.dot(h, wk_ref[...], preferred_element_type=F32).astype(BF16)
    r = lax.dot_general(w3t_ref[...], h, (((1,), (1,)), ((), ())), preferred_element_type=F32)
    w = SB_WIDTH
    qt_ref[0, 0] = (r[0:w] * (SB_HEAD_DIM ** -0.5)).astype(BF16)
    vt_ref[0, 0] = r[w:2 * w].astype(BF16)
    gt_ref[0, 0] = r[2 * w:3 * w]


def _sb_inproj(x1, g, wk, w3t):
    bsz, s, d = x1.shape
    nb = s // SB_T
    w = SB_WIDTH
    const = lambda b, t: (0, 0)
    tspec = pl.BlockSpec((1, 1, w, SB_T), lambda b, t: (b, t, 0, 0))
    return pl.pallas_call(
        _sb_inproj_kernel,
        grid=(bsz, nb),
        in_specs=[
            pl.BlockSpec((1, SB_T, d), lambda b, t: (b, t, 0)),
            pl.BlockSpec((1, d), const),
            pl.BlockSpec((d, w), const),
            pl.BlockSpec((3 * w, d), const),
        ],
        out_specs=[pl.BlockSpec((1, SB_T, w), lambda b, t: (b, t, 0)), tspec, tspec, tspec],
        out_shape=[
            jax.ShapeDtypeStruct((bsz, s, w), BF16),
            jax.ShapeDtypeStruct((bsz, nb, w, SB_T), BF16),
            jax.ShapeDtypeStruct((bsz, nb, w, SB_T), BF16),
            jax.ShapeDtypeStruct((bsz, nb, w, SB_T), F32),
        ],
        compiler_params=pltpu.CompilerParams(
            dimension_semantics=("arbitrary", "arbitrary"), vmem_limit_bytes=VMEM_LIMIT),
        name="sb_inproj",
    )(x1, g, wk, w3t)


def _sb_tile(kblk, vblk, qpad, ut, carry, acc, mask):
    zt = jnp.dot(kblk, qpad, preferred_element_type=F32)
    sp = _softplus(zt)
    if mask is not None:
        sp = jnp.where(mask, sp, 0.0)
    hi = sp.astype(BF16)
    lo = (sp - hi.astype(F32)).astype(BF16)
    ct = jnp.dot(ut, hi, preferred_element_type=F32) + jnp.dot(ut, lo, preferred_element_type=F32)
    p = jnp.exp(zt - ct - carry[0:1, :])
    if mask is not None:
        p = jnp.where(mask, p, 0.0)
    acc = acc + jnp.dot(vblk, p.astype(BF16), preferred_element_type=F32)
    carry = carry + jnp.broadcast_to(ct[0:1, :], carry.shape)
    return carry, acc


def _sb_attn_kernel(k_ref, qt_ref, vt_ref, gt_ref, ut_ref, yt_ref):
    qi = pl.program_id(2)
    t = SB_T
    dh = SB_HEAD_DIM
    ut = ut_ref[...]
    qblk = qt_ref[0, 0]
    rows = lax.broadcasted_iota(jnp.int32, (HEAD_PAIR, t), 0)
    qpads = [jnp.where((rows >= dh * hh) & (rows < dh * (hh + 1)), qblk, jnp.zeros_like(qblk))
             for hh in range(2)]
    causal = (lax.broadcasted_iota(jnp.int32, (t, t), 0) < lax.broadcasted_iota(jnp.int32, (t, t), 1))

    def step(kj, state, mask):
        kblk = k_ref[0, pl.ds(pl.multiple_of(kj * t, t), t), :]
        vpair = vt_ref[0, kj]
        new = []
        for hh in range(2):
            carry, acc = state[2 * hh], state[2 * hh + 1]
            new.extend(_sb_tile(kblk, vpair[dh * hh:dh * (hh + 1), :], qpads[hh], ut, carry, acc, mask))
        return tuple(new)

    zeros = (jnp.zeros((SUBLANES, t), F32), jnp.zeros((dh, t), F32))
    state = step(qi, zeros + zeros, causal)
    state = lax.fori_loop(0, qi, lambda it, st: step(qi - 1 - it, st, None), state)

    for hh in range(2):
        g = gt_ref[0, 0, dh * hh:dh * (hh + 1), :]
        yt_ref[0, 0, dh * hh:dh * (hh + 1), :] = (state[2 * hh + 1] * (g * jax.nn.sigmoid(g))).astype(BF16)


def _sb_attn(k, qt, vt, gt, ut):
    bsz, s, w = k.shape
    nb = s // SB_T
    blk = pl.BlockSpec((1, 1, HEAD_PAIR, SB_T), lambda b, j, i: (b, i, j, 0))
    return pl.pallas_call(
        _sb_attn_kernel,
        grid=(bsz, w // HEAD_PAIR, nb),
        in_specs=[
            pl.BlockSpec((1, s, HEAD_PAIR), lambda b, j, i: (b, 0, j)),
            blk,
            pl.BlockSpec((1, nb, HEAD_PAIR, SB_T), lambda b, j, i: (b, 0, j, 0)),
            blk,
            pl.BlockSpec((SB_T, SB_T), lambda b, j, i: (0, 0)),
        ],
        out_specs=blk,
        out_shape=jax.ShapeDtypeStruct((bsz, nb, w, SB_T), BF16),
        compiler_params=pltpu.CompilerParams(
            dimension_semantics=("arbitrary", "arbitrary", "arbitrary"), vmem_limit_bytes=VMEM_LIMIT),
        name="sb_attn",
    )(k, qt, vt, gt, ut)


def _sb_out_kernel(x_ref, yt_ref, wout_ref, g_ref, o_ref):
    y = lax.dot_general(yt_ref[0, 0], wout_ref[...], (((0,), (0,)), ((), ())), preferred_element_type=F32)
    o_ref[0] = _rms_norm(x_ref[0] + y, g_ref[...])


def _sb_out(x1, yt, w_out, g):
    bsz, s, d = x1.shape
    w = SB_WIDTH
    const = lambda b, t: (0, 0)
    xspec = pl.BlockSpec((1, SB_T, d), lambda b, t: (b, t, 0))
    return pl.pallas_call(
        _sb_out_kernel,
        grid=(bsz, s // SB_T),
        in_specs=[
            xspec,
            pl.BlockSpec((1, 1, w, SB_T), lambda b, t: (b, t, 0, 0)),
            pl.BlockSpec((w, d), const),
            pl.BlockSpec((1, d), const),
        ],
        out_specs=xspec,
        out_shape=jax.ShapeDtypeStruct((bsz, s, d), F32),
        compiler_params=pltpu.CompilerParams(
            dimension_semantics=("arbitrary", "arbitrary"), vmem_limit_bytes=VMEM_LIMIT),
        name="sb_out",
    )(x1, yt, w_out, g)


def _expand_gate_weights(w_gates, b_gates):
    eye = jnp.eye(LRU_HEADS, dtype=w_gates.dtype)
    halves, biases = [], []
    for lo in (0, LRU_BLOCK):
        blk = w_gates[:, :, lo:lo + LRU_BLOCK]
        dense = jnp.einsum("hio,hg->higo", blk, eye).reshape(LRU_WIDTH, LRU_WIDTH)
        halves.append(dense)
        biases.append(b_gates[:, lo:lo + LRU_BLOCK].reshape(1, LRU_WIDTH))
    return jnp.concatenate(halves, axis=1), jnp.concatenate(biases, axis=1)


def kernel(x, norm_g, final_norm_g, lru_w_in, lru_conv_w, lru_conv_b, lru_w_gates, lru_b_gates, lru_lambda,
           lru_w_out, sb_w_in, sb_w_out):
    w = SB_WIDTH
    wg, bg = _expand_gate_weights(lru_w_gates[0], lru_b_gates[0])
    x1 = _lru_layer(
        x, norm_g[0].reshape(1, D_MODEL), lru_w_in[0].astype(BF16), lru_conv_w[0],
        lru_conv_b[0].reshape(1, LRU_WIDTH), wg.astype(BF16), bg, lru_lambda[0].reshape(1, LRU_WIDTH),
        lru_w_out[0].astype(BF16))

    w_in = sb_w_in[0]
    wk = w_in[:, w:2 * w].astype(BF16)
    w3t = jnp.concatenate([w_in[:, 0:w], w_in[:, 2 * w:3 * w], w_in[:, 3 * w:4 * w]], axis=1).T.astype(BF16)
    k, qt, vt, gt = _sb_inproj(x1, norm_g[1].reshape(1, D_MODEL), wk, w3t)

    idx = jnp.arange(SB_T)
    ut = (idx[None, :] >= idx[:, None]).astype(BF16)
    yt = _sb_attn(k, qt, vt, gt, ut)
    return _sb_out(x1, yt, sb_w_out[0].astype(BF16), final_norm_g.reshape(1, D_MODEL))
```

```python
import jax
import jax.numpy as jnp
from jax import lax
from jax.experimental import pallas as pl
from jax.experimental.pallas import tpu as pltpu

F32 = jnp.float32
BF16 = jnp.bfloat16

D_MODEL = 1024
LRU_HEADS = 16
LRU_BLOCK = 80
LRU_WIDTH = LRU_HEADS * LRU_BLOCK
LRU_C = 8.0
CONV_WIDTH = 4
SB_HEAD_DIM = 64
SB_HEADS = 16
SB_WIDTH = SB_HEADS * SB_HEAD_DIM
RMS_EPS = 1e-6

SUBLANES = 8
LANES = 128
MXU_DIM = 256
GATE_WIN = 2 * MXU_DIM
LRU_TS = 256
SB_T = 256
HEAD_PAIR = 2 * SB_HEAD_DIM
SB_PAIRS = 4
LOG2E = 1.4426950408889634
VMEM_LIMIT = 56 * 1024 * 1024


def _rms_norm(x, g):
    ms = jnp.mean(x * x, axis=-1, keepdims=True)
    return x * lax.rsqrt(ms + RMS_EPS) * g


def _lru_layer_kernel(x_ref, g_ref, win_ref, cw_ref, cb_ref, wg_ref, bg_ref, lam_ref, wout_ref,
                      o_ref, xpad_ref, hprev_ref, a_ref, b_ref):
    ts = LRU_TS
    c = LRU_WIDTH

    @pl.when(pl.program_id(1) == 0)
    def _():
        xpad_ref[0:SUBLANES, :] = jnp.zeros((SUBLANES, c), F32)
        hprev_ref[...] = jnp.zeros((SUBLANES, c), F32)

    xt = x_ref[0]
    h = _rms_norm(xt, g_ref[...]).astype(BF16)
    u = jnp.dot(h, win_ref[...], preferred_element_type=F32)
    gate = u[:, c:]

    xpad_ref[SUBLANES:SUBLANES + ts, :] = u[:, :c]
    cw = cw_ref[...]
    xc = cb_ref[...]
    for tap in range(CONV_WIDTH):
        off = SUBLANES - (CONV_WIDTH - 1) + tap
        xc = xc + cw[tap:tap + 1, :] * xpad_ref[off:off + ts, :]
    xpad_ref[0:SUBLANES, :] = xpad_ref[ts:ts + SUBLANES, :]

    xcb = xc.astype(BF16)
    parts = [jnp.dot(xcb[:, ks:ks + GATE_WIN], wg_ref[n], preferred_element_type=F32)
             for n, ks in enumerate(_gate_windows())]
    bg = bg_ref[...]
    r = jax.nn.sigmoid(jnp.concatenate([p[:, :MXU_DIM] for p in parts], axis=1) + bg[:, :c])
    i = jax.nn.sigmoid(jnp.concatenate([p[:, MXU_DIM:] for p in parts], axis=1) + bg[:, c:])
    lam = lam_ref[...]
    log_sig_lam = jnp.minimum(lam, 0.0) - jnp.log(1.0 + jnp.exp(-jnp.abs(lam)))
    log_a = (LRU_C * r) * log_sig_lam
    a = jnp.exp(log_a)
    a_ref[...] = a
    b_ref[...] = jnp.sqrt(jnp.tanh(-log_a) * (1.0 + a * a)) * (i * xc)

    row = lax.broadcasted_iota(jnp.int32, (SUBLANES, c), 0)

    def scan_group(gidx, hprev):
        off = pl.multiple_of(gidx * SUBLANES, SUBLANES)
        av = a_ref[pl.ds(off, SUBLANES), :]
        bv = b_ref[pl.ds(off, SUBLANES), :]
        for k in (1, 2, 4):
            a_sh = jnp.where(row >= k, pltpu.roll(av, k, 0), 1.0)
            b_sh = jnp.where(row >= k, pltpu.roll(bv, k, 0), 0.0)
            bv = av * b_sh + bv
            av = av * a_sh
        hv = av * hprev + bv
        b_ref[pl.ds(off, SUBLANES), :] = hv
        return jnp.broadcast_to(hv[SUBLANES - 1:SUBLANES, :], (SUBLANES, c))

    hprev_ref[...] = lax.fori_loop(0, ts // SUBLANES, scan_group, hprev_ref[...], unroll=4)

    y = (b_ref[...] * (gate * jax.nn.sigmoid(gate))).astype(BF16)
    o_ref[0] = xt + jnp.dot(y, wout_ref[...], preferred_element_type=F32)


def _lru_layer(x, g, w_in, conv_w, conv_b, w_gates, b_gates, lam, w_out):
    bsz, s, d = x.shape
    c = LRU_WIDTH
    const = lambda b, t: (0, 0)
    return pl.pallas_call(
        _lru_layer_kernel,
        grid=(bsz, s // LRU_TS),
        in_specs=[
            pl.BlockSpec((1, LRU_TS, d), lambda b, t: (b, t, 0)),
            pl.BlockSpec((1, d), const),
            pl.BlockSpec((d, 2 * c), const),
            pl.BlockSpec((CONV_WIDTH, c), const),
            pl.BlockSpec((1, c), const),
            pl.BlockSpec((c // MXU_DIM, GATE_WIN, 2 * MXU_DIM), lambda b, t: (0, 0, 0)),
            pl.BlockSpec((1, 2 * c), const),
            pl.BlockSpec((1, c), const),
            pl.BlockSpec((c, d), const),
        ],
        out_specs=pl.BlockSpec((1, LRU_TS, d), lambda b, t: (b, t, 0)),
        out_shape=jax.ShapeDtypeStruct((bsz, s, d), F32),
        scratch_shapes=[
            pltpu.VMEM((LRU_TS + SUBLANES, c), F32),
            pltpu.VMEM((SUBLANES, c), F32),
            pltpu.VMEM((LRU_TS, c), F32),
            pltpu.VMEM((LRU_TS, c), F32),
        ],
        compiler_params=pltpu.CompilerParams(
            dimension_semantics=("arbitrary", "arbitrary"), vmem_limit_bytes=VMEM_LIMIT),
        name="lru_layer",
    )(x, g, w_in, conv_w, conv_b, w_gates, b_gates, lam, w_out)


def _sb_inproj_kernel(x_ref, g_ref, wk_ref, w3t_ref, k_ref, qt_ref, vt_ref, gt_ref):
    h = _rms_norm(x_ref[0], g_ref[...]).astype(BF16)
    k_ref[0] = jnp.dot(h, wk_ref[...], preferred_element_type=F32).astype(BF16)
    r = lax.dot_general(w3t_ref[...], h, (((1,), (1,)), ((), ())), preferred_element_type=F32)
    w = SB_WIDTH
    qt_ref[0, 0] = (r[0:w] * (SB_HEAD_DIM ** -0.5 * LOG2E)).astype(BF16)
    vt_ref[0, 0] = r[w:2 * w].astype(BF16)
    gt_ref[0, 0] = r[2 * w:3 * w]


def _sb_inproj(x1, g, wk, w3t):
    bsz, s, d = x1.shape
    nb = s // SB_T
    w = SB_WIDTH
    const = lambda b, t: (0, 0)
    tspec = pl.BlockSpec((1, 1, w, SB_T), lambda b, t: (b, t, 0, 0))
    return pl.pallas_call(
        _sb_inproj_kernel,
        grid=(bsz, nb),
        in_specs=[
            pl.BlockSpec((1, SB_T, d), lambda b, t: (b, t, 0)),
            pl.BlockSpec((1, d), const),
            pl.BlockSpec((d, w), const),
            pl.BlockSpec((3 * w, d), const),
        ],
        out_specs=[pl.BlockSpec((1, SB_T, w), lambda b, t: (b, t, 0)), tspec, tspec, tspec],
        out_shape=[
            jax.ShapeDtypeStruct((bsz, s, w), BF16),
            jax.ShapeDtypeStruct((bsz, nb, w, SB_T), BF16),
            jax.ShapeDtypeStruct((bsz, nb, w, SB_T), BF16),
            jax.ShapeDtypeStruct((bsz, nb, w, SB_T), F32),
        ],
        compiler_params=pltpu.CompilerParams(
            dimension_semantics=("arbitrary", "arbitrary"), vmem_limit_bytes=VMEM_LIMIT),
        name="sb_inproj",
    )(x1, g, wk, w3t)


def _softplus2(z):
    return jnp.maximum(z, 0.0) + jnp.log2(1.0 + jnp.exp2(-jnp.abs(z)))


def _sb_step(k_next, v_prev, qpads, ut, z_cur, z_next, p_scr, acc_scr, carries, mask):
    t = SB_T
    n = len(qpads)
    cts = []
    for h in range(n):
        sp = _softplus2(z_cur[h])
        if mask is not None:
            sp = jnp.where(mask, sp, 0.0)
        else:
            sp = jnp.concatenate([sp[:t - SUBLANES], sp[t - SUBLANES:] + carries[h]], axis=0)
        cts.append(jnp.dot(ut, sp.astype(BF16), preferred_element_type=F32))
        z_next[h] = jnp.dot(k_next[h], qpads[h], preferred_element_type=F32)
        if v_prev is not None:
            acc_scr[h] += jnp.dot(v_prev[h], p_scr[h], preferred_element_type=F32)
    last_row = lax.broadcasted_iota(jnp.int32, (SUBLANES, t), 0) == SUBLANES - 1
    carries_new = []
    for h in range(n):
        p = jnp.exp2(z_cur[h] - cts[h])
        if mask is not None:
            p = jnp.where(mask, p, 0.0)
        p_scr[h] = p.astype(BF16)
        carries_new.append(jnp.where(last_row, jnp.broadcast_to(cts[h][0:1, :], (SUBLANES, t)), 0.0))
    return tuple(carries_new)


def _sb_attn_kernel(k_ref, qt_ref, vt_ref, gt_ref, ut_ref, yt_ref, za_scr, zb_scr, p_scr, acc_scr):
    qi = pl.program_id(2)
    t = SB_T
    dh = SB_HEAD_DIM
    n = 2 * SB_PAIRS
    ut = ut_ref[...]
    rows = lax.broadcasted_iota(jnp.int32, (HEAD_PAIR, t), 0)
    qpads = []
    for hd in range(n):
        qblk = qt_ref[0, 0, HEAD_PAIR * (hd // 2):HEAD_PAIR * (hd // 2 + 1), :]
        lo_row = dh * (hd % 2)
        qpads.append(jnp.where((rows >= lo_row) & (rows < lo_row + dh), qblk, jnp.zeros_like(qblk)))
    causal = (lax.broadcasted_iota(jnp.int32, (t, t), 0) < lax.broadcasted_iota(jnp.int32, (t, t), 1))

    def kblks(kj):
        koff = pl.multiple_of(jnp.maximum(kj, 0) * t, t)
        return [k_ref[0, pl.ds(koff, t), HEAD_PAIR * (hd // 2):HEAD_PAIR * (hd // 2 + 1)] for hd in range(n)]

    def vblks(kj):
        return [vt_ref[0, kj, dh * hd:dh * (hd + 1), :] for hd in range(n)]

    def step(cur, z_cur, z_next, carries, mask=None):
        v_prev = None if mask is not None else vblks(cur + 1)
        return _sb_step(kblks(cur - 1), v_prev, qpads, ut, z_cur, z_next, p_scr, acc_scr, carries, mask)

    for hd, kb in enumerate(kblks(qi)):
        zb_scr[hd] = jnp.dot(kb, qpads[hd], preferred_element_type=F32)
        acc_scr[hd] = jnp.zeros((dh, t), F32)
    carries = step(qi, zb_scr, za_scr, None, causal)

    def pair(i, carries):
        cur = qi - 1 - 2 * i
        carries = step(cur, za_scr, zb_scr, carries)
        return step(cur - 1, zb_scr, za_scr, carries)

    carries = lax.fori_loop(0, qi // 2, pair, carries)

    @pl.when(qi % 2 == 1)
    def _():
        step(0, za_scr, zb_scr, carries)

    v_last = vblks(0)
    for hd in range(n):
        acc = acc_scr[hd] + jnp.dot(v_last[hd], p_scr[hd], preferred_element_type=F32)
        g = gt_ref[0, 0, dh * hd:dh * (hd + 1), :]
        yt_ref[0, 0, dh * hd:dh * (hd + 1), :] = (acc * (g * jax.nn.sigmoid(g))).astype(BF16)


def _sb_attn(k, qt, vt, gt, ut):
    bsz, s, w = k.shape
    nb = s // SB_T
    gw = HEAD_PAIR * SB_PAIRS
    blk = pl.BlockSpec((1, 1, gw, SB_T), lambda b, j, i: (b, i, j, 0))
    return pl.pallas_call(
        _sb_attn_kernel,
        grid=(bsz, w // gw, nb),
        in_specs=[
            pl.BlockSpec((1, s, gw), lambda b, j, i: (b, 0, j)),
            blk,
            pl.BlockSpec((1, nb, gw, SB_T), lambda b, j, i: (b, 0, j, 0)),
            blk,
            pl.BlockSpec((SB_T, SB_T), lambda b, j, i: (0, 0)),
        ],
        out_specs=blk,
        out_shape=jax.ShapeDtypeStruct((bsz, nb, w, SB_T), BF16),
        scratch_shapes=[
            pltpu.VMEM((2 * SB_PAIRS, SB_T, SB_T), F32),
            pltpu.VMEM((2 * SB_PAIRS, SB_T, SB_T), F32),
            pltpu.VMEM((2 * SB_PAIRS, SB_T, SB_T), BF16),
            pltpu.VMEM((2 * SB_PAIRS, SB_HEAD_DIM, SB_T), F32),
        ],
        compiler_params=pltpu.CompilerParams(
            dimension_semantics=("arbitrary", "arbitrary", "arbitrary"), vmem_limit_bytes=VMEM_LIMIT),
        name="sb_attn",
    )(k, qt, vt, gt, ut)


def _sb_out_kernel(x_ref, yt_ref, wout_ref, g_ref, o_ref):
    y = lax.dot_general(yt_ref[0, 0], wout_ref[...], (((0,), (0,)), ((), ())), preferred_element_type=F32)
    o_ref[0] = _rms_norm(x_ref[0] + y, g_ref[...])


def _sb_out(x1, yt, w_out, g):
    bsz, s, d = x1.shape
    w = SB_WIDTH
    const = lambda b, t: (0, 0)
    xspec = pl.BlockSpec((1, SB_T, d), lambda b, t: (b, t, 0))
    return pl.pallas_call(
        _sb_out_kernel,
        grid=(bsz, s // SB_T),
        in_specs=[
            xspec,
            pl.BlockSpec((1, 1, w, SB_T), lambda b, t: (b, t, 0, 0)),
            pl.BlockSpec((w, d), const),
            pl.BlockSpec((1, d), const),
        ],
        out_specs=xspec,
        out_shape=jax.ShapeDtypeStruct((bsz, s, d), F32),
        compiler_params=pltpu.CompilerParams(
            dimension_semantics=("arbitrary", "arbitrary"), vmem_limit_bytes=VMEM_LIMIT),
        name="sb_out",
    )(x1, yt, w_out, g)


def _gate_windows():
    starts = []
    for n in range(LRU_WIDTH // MXU_DIM):
        first_blk = (MXU_DIM * n) // LRU_BLOCK
        last_blk = (MXU_DIM * (n + 1) - 1) // LRU_BLOCK
        lo, hi = LRU_BLOCK * first_blk, LRU_BLOCK * (last_blk + 1)
        ks = min(lo // LANES * LANES, LRU_WIDTH - GATE_WIN)
        assert ks <= lo and hi <= ks + GATE_WIN
        starts.append(ks)
    return starts


def _expand_gate_weights(w_gates, b_gates):
    eye = jnp.eye(LRU_HEADS, dtype=w_gates.dtype)
    halves, biases = [], []
    for lo in (0, LRU_BLOCK):
        blk = w_gates[:, :, lo:lo + LRU_BLOCK]
        halves.append(jnp.einsum("hio,hg->higo", blk, eye).reshape(LRU_WIDTH, LRU_WIDTH))
        biases.append(b_gates[:, lo:lo + LRU_BLOCK].reshape(1, LRU_WIDTH))
    tiles = [jnp.concatenate([h[ks:ks + GATE_WIN, MXU_DIM * n:MXU_DIM * (n + 1)] for h in halves], axis=1)
             for n, ks in enumerate(_gate_windows())]
    return jnp.stack(tiles), jnp.concatenate(biases, axis=1)


def kernel(x, norm_g, final_norm_g, lru_w_in, lru_conv_w, lru_conv_b, lru_w_gates, lru_b_gates, lru_lambda,
           lru_w_out, sb_w_in, sb_w_out):
    w = SB_WIDTH
    wg, bg = _expand_gate_weights(lru_w_gates[0], lru_b_gates[0])
    x1 = _lru_layer(
        x, norm_g[0].reshape(1, D_MODEL), lru_w_in[0].astype(BF16), lru_conv_w[0],
        lru_conv_b[0].reshape(1, LRU_WIDTH), wg.astype(BF16), bg, lru_lambda[0].reshape(1, LRU_WIDTH),
        lru_w_out[0].astype(BF16))

    w_in = sb_w_in[0]
    wk = w_in[:, w:2 * w].astype(BF16)
    w3t = jnp.concatenate([w_in[:, 0:w], w_in[:, 2 * w:3 * w], w_in[:, 3 * w:4 * w]], axis=1).T.astype(BF16)
    k, qt, vt, gt = _sb_inproj(x1, norm_g[1].reshape(1, D_MODEL), wk, w3t)

    idx = jnp.arange(SB_T)
    ut = (idx[None, :] >= idx[:, None]).astype(BF16)
    yt = _sb_attn(k, qt, vt, gt, ut)
    return _sb_out(x1, yt, sb_w_out[0].astype(BF16), final_norm_g.reshape(1, D_MODEL))
```

```python
import jax
import jax.numpy as jnp
from jax import lax
from jax.experimental import pallas as pl
from jax.experimental.pallas import tpu as pltpu

F32 = jnp.float32
BF16 = jnp.bfloat16

D_MODEL = 1024
LRU_HEADS = 16
LRU_BLOCK = 80
LRU_WIDTH = LRU_HEADS * LRU_BLOCK
LRU_C = 8.0
CONV_WIDTH = 4
SB_HEAD_DIM = 64
SB_HEADS = 16
SB_WIDTH = SB_HEADS * SB_HEAD_DIM
RMS_EPS = 1e-6

SUBLANES = 8
LANES = 128
BF16_ROWS = 16
MXU_DIM = 256
GATE_WIN = 2 * MXU_DIM
LRU_TS = 256
SB_T = 256
HEAD_PAIR = 2 * SB_HEAD_DIM
SB_HALF = SB_T // 2
SB_PAIRS = 4
SB_OUT_BLOCKS = 2
LOG2E = 1.4426950408889634
MASKED = -1e30
VMEM_LIMIT = 56 * 1024 * 1024


def _rms_norm(x, g):
    ms = jnp.mean(x * x, axis=-1, keepdims=True)
    return x * lax.rsqrt(ms + RMS_EPS) * g


def _lru_layer_kernel(x_ref, g_ref, win_ref, cw_ref, cb_ref, wg_ref, bg_ref, lam_ref, wout_ref,
                      o_ref, xpad_ref, hprev_ref, a_ref, b_ref):
    ts = LRU_TS
    c = LRU_WIDTH

    @pl.when(pl.program_id(1) == 0)
    def _():
        xpad_ref[0:SUBLANES, :] = jnp.zeros((SUBLANES, c), F32)
        hprev_ref[...] = jnp.zeros((SUBLANES, c), F32)

    xt = x_ref[0]
    h = _rms_norm(xt, g_ref[...]).astype(BF16)
    u = jnp.dot(h, win_ref[...], preferred_element_type=F32)
    half_gate = u[:, c:]

    xpad_ref[SUBLANES:SUBLANES + ts, :] = u[:, :c]
    xp = xpad_ref[...]
    cw = cw_ref[...]
    xc = cb_ref[...] + cw[CONV_WIDTH - 1:CONV_WIDTH, :] * xp[SUBLANES:SUBLANES + ts, :]
    for back in range(1, CONV_WIDTH):
        tap = CONV_WIDTH - 1 - back
        xc = xc + cw[tap:tap + 1, :] * pltpu.roll(xp, back, 0)[SUBLANES:SUBLANES + ts, :]
    xpad_ref[0:SUBLANES, :] = xp[ts:ts + SUBLANES, :]

    xcb = [xc[:, LANES * j:LANES * (j + 1)].astype(BF16) for j in range(c // LANES)]
    parts = [jnp.dot(jnp.concatenate(xcb[ks // LANES:(ks + GATE_WIN) // LANES], axis=1), wg_ref[n],
                     preferred_element_type=F32)
             for n, ks in enumerate(_gate_windows())]
    half_bg = bg_ref[...]
    tr = jnp.tanh(jnp.concatenate([p[:, :MXU_DIM] for p in parts], axis=1) + half_bg[:, :c])
    ti = jnp.tanh(jnp.concatenate([p[:, MXU_DIM:] for p in parts], axis=1) + half_bg[:, c:])
    lam = lam_ref[...]
    log_sig_lam = jnp.minimum(lam, 0.0) - jnp.log(1.0 + jnp.exp(-jnp.abs(lam)))
    half_c = (0.5 * LRU_C) * log_sig_lam
    log_a = half_c * tr + half_c
    a = jnp.exp(log_a)
    a_ref[...] = a
    m2 = jnp.tanh(-log_a) * (1.0 + a * a)
    mult = jnp.where(m2 > 0.0, m2 * lax.rsqrt(m2), 0.0)
    half_xc = 0.5 * xc
    b_ref[...] = mult * (half_xc * ti + half_xc)

    row = lax.broadcasted_iota(jnp.int32, (SUBLANES, c), 0)

    def scan_group(gidx, hprev):
        off = pl.multiple_of(gidx * SUBLANES, SUBLANES)
        av = a_ref[pl.ds(off, SUBLANES), :]
        bv = b_ref[pl.ds(off, SUBLANES), :]
        for k in (1, 2, 4):
            a_sh = jnp.where(row >= k, pltpu.roll(av, k, 0), 1.0)
            b_sh = jnp.where(row >= k, pltpu.roll(bv, k, 0), 0.0)
            bv = av * b_sh + bv
            av = av * a_sh
        hv = av * hprev + bv
        b_ref[pl.ds(off, SUBLANES), :] = hv
        return jnp.broadcast_to(hv[SUBLANES - 1:SUBLANES, :], (SUBLANES, c))

    hprev_ref[...] = lax.fori_loop(0, ts // SUBLANES, scan_group, hprev_ref[...], unroll=4)

    silu_gate = half_gate * jnp.tanh(half_gate) + half_gate
    y = (b_ref[...] * silu_gate).astype(BF16)
    o_ref[0] = xt + jnp.dot(y, wout_ref[...], preferred_element_type=F32)


def _lru_layer(x, g, w_in, conv_w, conv_b, w_gates, b_gates, lam, w_out):
    bsz, s, d = x.shape
    c = LRU_WIDTH
    const = lambda b, t: (0, 0)
    return pl.pallas_call(
        _lru_layer_kernel,
        grid=(bsz, s // LRU_TS),
        in_specs=[
            pl.BlockSpec((1, LRU_TS, d), lambda b, t: (b, t, 0)),
            pl.BlockSpec((1, d), const),
            pl.BlockSpec((d, 2 * c), const),
            pl.BlockSpec((CONV_WIDTH, c), const),
            pl.BlockSpec((1, c), const),
            pl.BlockSpec((c // MXU_DIM, GATE_WIN, 2 * MXU_DIM), lambda b, t: (0, 0, 0)),
            pl.BlockSpec((1, 2 * c), const),
            pl.BlockSpec((1, c), const),
            pl.BlockSpec((c, d), const),
        ],
        out_specs=pl.BlockSpec((1, LRU_TS, d), lambda b, t: (b, t, 0)),
        out_shape=jax.ShapeDtypeStruct((bsz, s, d), F32),
        scratch_shapes=[
            pltpu.VMEM((LRU_TS + SUBLANES, c), F32),
            pltpu.VMEM((SUBLANES, c), F32),
            pltpu.VMEM((LRU_TS, c), F32),
            pltpu.VMEM((LRU_TS, c), F32),
        ],
        compiler_params=pltpu.CompilerParams(
            dimension_semantics=("arbitrary", "arbitrary"), vmem_limit_bytes=VMEM_LIMIT),
        name="lru_layer",
    )(x, g, w_in, conv_w, conv_b, w_gates, b_gates, lam, w_out)


def _sb_inproj_kernel(x_ref, g_ref, wk_ref, w3t_ref, k_ref, qt_ref, vt_ref, gt_ref):
    h = _rms_norm(x_ref[0], g_ref[...]).astype(BF16)
    k_ref[0] = jnp.dot(h, wk_ref[...], preferred_element_type=F32).astype(BF16)
    r = lax.dot_general(w3t_ref[...], h, (((1,), (1,)), ((), ())), preferred_element_type=F32)
    w = SB_WIDTH
    qt_ref[0, 0] = (r[0:w] * (SB_HEAD_DIM ** -0.5 * LOG2E)).astype(BF16)
    vt_ref[0, 0] = r[w:2 * w].astype(BF16)
    gt_ref[0, 0] = r[2 * w:3 * w]


def _sb_inproj(x1, g, wk, w3t):
    bsz, s, d = x1.shape
    nb = s // SB_T
    w = SB_WIDTH
    const = lambda b, t: (0, 0)
    tspec = pl.BlockSpec((1, 1, w, SB_T), lambda b, t: (b, t, 0, 0))
    return pl.pallas_call(
        _sb_inproj_kernel,
        grid=(bsz, nb),
        in_specs=[
            pl.BlockSpec((1, SB_T, d), lambda b, t: (b, t, 0)),
            pl.BlockSpec((1, d), const),
            pl.BlockSpec((d, w), const),
            pl.BlockSpec((3 * w, d), const),
        ],
        out_specs=[pl.BlockSpec((1, SB_T, w), lambda b, t: (b, t, 0)), tspec, tspec, tspec],
        out_shape=[
            jax.ShapeDtypeStruct((bsz, s, w), BF16),
            jax.ShapeDtypeStruct((bsz, nb, w, SB_T), BF16),
            jax.ShapeDtypeStruct((bsz, nb, w, SB_T), BF16),
            jax.ShapeDtypeStruct((bsz, nb, w, SB_T), F32),
        ],
        compiler_params=pltpu.CompilerParams(
            dimension_semantics=("arbitrary", "arbitrary"), vmem_limit_bytes=VMEM_LIMIT),
        name="sb_inproj",
    )(x1, g, wk, w3t)


def _softplus2(z):
    return jnp.maximum(z, 0.0) + jnp.log2(1.0 + jnp.exp2(-jnp.abs(z)))


def _sb_step(k_cur, k_next, v_prev, qpads, neg_ut, z_cur, z_next, p_scr, acc_scr, carries, mask):
    t = SB_T
    hk = SB_HALF
    n = len(qpads)
    last_row = lax.broadcasted_iota(jnp.int32, (BF16_ROWS, t), 0) == BF16_ROWS - 1

    def half_block(h, lo, carry):
        z = z_cur[h, lo:lo + hk, :]
        zm = z if mask is None else jnp.where(mask[lo:lo + hk, :], z, MASKED)
        sp = _softplus2(zm)
        if carry is not None:
            sp = jnp.concatenate([sp[:hk - BF16_ROWS], sp[hk - BF16_ROWS:] + carry], axis=0)
        lhs = jnp.concatenate([neg_ut, k_cur[h][lo:lo + hk, :]], axis=1)
        rhs = jnp.concatenate([sp.astype(BF16), qpads[h]], axis=0)
        w = jnp.dot(lhs, rhs, preferred_element_type=F32)
        total = z[0:1, :] - w[0:1, :]
        return w, jnp.where(last_row, jnp.broadcast_to(total, (BF16_ROWS, t)), 0.0)

    w_late, mid = [], []
    for h in range(n):
        w, c = half_block(h, hk, None if mask is not None else carries[h])
        w_late.append(w)
        mid.append(c)
        z_next[h] = jnp.dot(k_next[h], qpads[h], preferred_element_type=F32)
        if v_prev is not None:
            acc_scr[h] += jnp.dot(v_prev[h], p_scr[h], preferred_element_type=F32)
    carries_new = []
    for h in range(n):
        w_early, c = half_block(h, 0, mid[h])
        carries_new.append(c)
        w = jnp.concatenate([w_early, w_late[h]], axis=0)
        if mask is not None:
            w = jnp.where(mask, w, MASKED)
        p_scr[h] = jnp.exp2(w).astype(BF16)
    return tuple(carries_new)


def _sb_attn_kernel(k_ref, qt_ref, vt_ref, gt_ref, nut_ref, yt_ref, za_scr, zb_scr, p_scr, acc_scr):
    qi = pl.program_id(2)
    t = SB_T
    dh = SB_HEAD_DIM
    n = 2 * SB_PAIRS
    neg_ut = nut_ref[...]
    rows = lax.broadcasted_iota(jnp.int32, (HEAD_PAIR, t), 0)
    qpads = []
    for hd in range(n):
        qblk = qt_ref[0, 0, HEAD_PAIR * (hd // 2):HEAD_PAIR * (hd // 2 + 1), :]
        lo_row = dh * (hd % 2)
        qpads.append(jnp.where((rows >= lo_row) & (rows < lo_row + dh), qblk, jnp.zeros_like(qblk)))
    causal = (lax.broadcasted_iota(jnp.int32, (t, t), 0) < lax.broadcasted_iota(jnp.int32, (t, t), 1))

    def kblks(kj):
        koff = pl.multiple_of(jnp.maximum(kj, 0) * t, t)
        return [k_ref[0, pl.ds(koff, t), HEAD_PAIR * (hd // 2):HEAD_PAIR * (hd // 2 + 1)] for hd in range(n)]

    def vblks(kj):
        return [vt_ref[0, kj, dh * hd:dh * (hd + 1), :] for hd in range(n)]

    def step(cur, z_cur, z_next, carries, mask=None):
        v_prev = None if mask is not None else vblks(cur + 1)
        return _sb_step(kblks(cur), kblks(cur - 1), v_prev, qpads, neg_ut, z_cur, z_next, p_scr, acc_scr,
                        carries, mask)

    for hd, kb in enumerate(kblks(qi)):
        zb_scr[hd] = jnp.dot(kb, qpads[hd], preferred_element_type=F32)
        acc_scr[hd] = jnp.zeros((dh, t), F32)
    carries = step(qi, zb_scr, za_scr, None, causal)

    def pair(i, carries):
        cur = qi - 1 - 2 * i
        carries = step(cur, za_scr, zb_scr, carries)
        return step(cur - 1, zb_scr, za_scr, carries)

    carries = lax.fori_loop(0, qi // 2, pair, carries)

    @pl.when(qi % 2 == 1)
    def _():
        step(0, za_scr, zb_scr, carries)

    v_last = vblks(0)
    for hd in range(n):
        acc = acc_scr[hd] + jnp.dot(v_last[hd], p_scr[hd], preferred_element_type=F32)
        g = gt_ref[0, 0, dh * hd:dh * (hd + 1), :]
        yt_ref[0, 0, dh * hd:dh * (hd + 1), :] = (acc * (g * jax.nn.sigmoid(g))).astype(BF16)


def _sb_attn(k, qt, vt, gt, neg_ut):
    bsz, s, w = k.shape
    nb = s // SB_T
    gw = HEAD_PAIR * SB_PAIRS
    blk = pl.BlockSpec((1, 1, gw, SB_T), lambda b, j, i: (b, i, j, 0))
    return pl.pallas_call(
        _sb_attn_kernel,
        grid=(bsz, w // gw, nb),
        in_specs=[
            pl.BlockSpec((1, s, gw), lambda b, j, i: (b, 0, j)),
            blk,
            pl.BlockSpec((1, nb, gw, SB_T), lambda b, j, i: (b, 0, j, 0)),
            blk,
            pl.BlockSpec((SB_HALF, SB_HALF), lambda b, j, i: (0, 0)),
        ],
        out_specs=blk,
        out_shape=jax.ShapeDtypeStruct((bsz, nb, w, SB_T), BF16),
        scratch_shapes=[
            pltpu.VMEM((2 * SB_PAIRS, SB_T, SB_T), F32),
            pltpu.VMEM((2 * SB_PAIRS, SB_T, SB_T), F32),
            pltpu.VMEM((2 * SB_PAIRS, SB_T, SB_T), BF16),
            pltpu.VMEM((2 * SB_PAIRS, SB_HEAD_DIM, SB_T), F32),
        ],
        compiler_params=pltpu.CompilerParams(
            dimension_semantics=("arbitrary", "arbitrary", "arbitrary"), vmem_limit_bytes=VMEM_LIMIT),
        name="sb_attn",
    )(k, qt, vt, gt, neg_ut)


def _sb_out_kernel(x_ref, yt_ref, wout_ref, g_ref, o_ref):
    for blk in range(SB_OUT_BLOCKS):
        rows = slice(SB_T * blk, SB_T * (blk + 1))
        y = lax.dot_general(yt_ref[0, blk], wout_ref[...], (((0,), (0,)), ((), ())), preferred_element_type=F32)
        o_ref[0, rows, :] = _rms_norm(x_ref[0, rows, :] + y, g_ref[...])


def _sb_out(x1, yt, w_out, g):
    bsz, s, d = x1.shape
    w = SB_WIDTH
    const = lambda b, t: (0, 0)
    xspec = pl.BlockSpec((1, SB_OUT_BLOCKS * SB_T, d), lambda b, t: (b, t, 0))
    return pl.pallas_call(
        _sb_out_kernel,
        grid=(bsz, s // (SB_OUT_BLOCKS * SB_T)),
        in_specs=[
            xspec,
            pl.BlockSpec((1, SB_OUT_BLOCKS, w, SB_T), lambda b, t: (b, t, 0, 0)),
            pl.BlockSpec((w, d), const),
            pl.BlockSpec((1, d), const),
        ],
        out_specs=xspec,
        out_shape=jax.ShapeDtypeStruct((bsz, s, d), F32),
        compiler_params=pltpu.CompilerParams(
            dimension_semantics=("arbitrary", "arbitrary"), vmem_limit_bytes=VMEM_LIMIT),
        name="sb_out",
    )(x1, yt, w_out, g)


def _gate_windows():
    starts = []
    for n in range(LRU_WIDTH // MXU_DIM):
        first_blk = (MXU_DIM * n) // LRU_BLOCK
        last_blk = (MXU_DIM * (n + 1) - 1) // LRU_BLOCK
        lo, hi = LRU_BLOCK * first_blk, LRU_BLOCK * (last_blk + 1)
        ks = min(lo // LANES * LANES, LRU_WIDTH - GATE_WIN)
        assert ks <= lo and hi <= ks + GATE_WIN
        starts.append(ks)
    return starts


def _expand_gate_weights(w_gates, b_gates):
    eye = jnp.eye(LRU_HEADS, dtype=w_gates.dtype)
    halves, biases = [], []
    for lo in (0, LRU_BLOCK):
        blk = w_gates[:, :, lo:lo + LRU_BLOCK]
        halves.append(jnp.einsum("hio,hg->higo", blk, eye).reshape(LRU_WIDTH, LRU_WIDTH))
        biases.append(b_gates[:, lo:lo + LRU_BLOCK].reshape(1, LRU_WIDTH))
    tiles = [jnp.concatenate([h[ks:ks + GATE_WIN, MXU_DIM * n:MXU_DIM * (n + 1)] for h in halves], axis=1)
             for n, ks in enumerate(_gate_windows())]
    return jnp.stack(tiles), jnp.concatenate(biases, axis=1)


def kernel(x, norm_g, final_norm_g, lru_w_in, lru_conv_w, lru_conv_b, lru_w_gates, lru_b_gates, lru_lambda,
           lru_w_out, sb_w_in, sb_w_out):
    w = SB_WIDTH
    wg, bg = _expand_gate_weights(lru_w_gates[0], lru_b_gates[0])
    col_scale = jnp.concatenate([jnp.ones((1, LRU_WIDTH), F32), jnp.full((1, LRU_WIDTH), 0.5, F32)], axis=1)
    x1 = _lru_layer(
        x, norm_g[0].reshape(1, D_MODEL), (lru_w_in[0] * col_scale).astype(BF16), lru_conv_w[0],
        lru_conv_b[0].reshape(1, LRU_WIDTH), (0.5 * wg).astype(BF16), 0.5 * bg,
        lru_lambda[0].reshape(1, LRU_WIDTH), lru_w_out[0].astype(BF16))

    w_in = sb_w_in[0]
    wk = w_in[:, w:2 * w].astype(BF16)
    w3t = jnp.concatenate([w_in[:, 0:w], w_in[:, 2 * w:3 * w], w_in[:, 3 * w:4 * w]], axis=1).T.astype(BF16)
    k, qt, vt, gt = _sb_inproj(x1, norm_g[1].reshape(1, D_MODEL), wk, w3t)

    idx = jnp.arange(SB_HALF)
    neg_ut = -(idx[None, :] >= idx[:, None]).astype(BF16)
    yt = _sb_attn(k, qt, vt, gt, neg_ut)
    return _sb_out(x1, yt, sb_w_out[0].astype(BF16), final_norm_g.reshape(1, D_MODEL))
```

```python
import jax
import jax.numpy as jnp
from jax import lax
from jax.experimental import pallas as pl
from jax.experimental.pallas import tpu as pltpu

F32 = jnp.float32
BF16 = jnp.bfloat16

D_MODEL = 1024
LRU_HEADS = 16
LRU_BLOCK = 80
LRU_WIDTH = LRU_HEADS * LRU_BLOCK
LRU_C = 8.0
CONV_WIDTH = 4
SB_HEAD_DIM = 64
SB_HEADS = 16
SB_WIDTH = SB_HEADS * SB_HEAD_DIM
RMS_EPS = 1e-6

SUBLANES = 8
LANES = 128
BF16_ROWS = 16
MXU_DIM = 256
GATE_WIN = 2 * MXU_DIM
LRU_TS = 256
LRU_OUT_CHUNKS = 2
SB_T = 256
HEAD_PAIR = 2 * SB_HEAD_DIM
SB_HALF = SB_T // 2
SB_PAIRS = 8
SB_OUT_BLOCKS = 2
LOG2E = 1.4426950408889634
MASKED = -1e30
VMEM_LIMIT = 56 * 1024 * 1024


def _rms_norm(x, g):
    ms = jnp.mean(x * x, axis=-1, keepdims=True)
    return x * lax.rsqrt(ms + RMS_EPS) * g


def _lru_layer_kernel(x_ref, g_ref, win_ref, cw_ref, cb_ref, wg_ref, bg_ref, lam_ref, wout_ref,
                      o_ref, xpad_ref, hprev_ref):
    ts = LRU_TS
    c = LRU_WIDTH

    @pl.when(pl.program_id(1) == 0)
    def _():
        xpad_ref[0:SUBLANES, :] = jnp.zeros((SUBLANES, c), F32)
        hprev_ref[...] = jnp.zeros((SUBLANES, c), F32)

    xt = x_ref[0]
    h = _rms_norm(xt, g_ref[...]).astype(BF16)
    u = jnp.dot(h, win_ref[...], preferred_element_type=F32)
    half_gate = u[:, c:]

    xpad_ref[SUBLANES:SUBLANES + ts, :] = u[:, :c]
    xp = xpad_ref[...]
    cw = cw_ref[...]
    xc = cb_ref[...] + cw[CONV_WIDTH - 1:CONV_WIDTH, :] * xp[SUBLANES:SUBLANES + ts, :]
    for back in range(1, CONV_WIDTH):
        tap = CONV_WIDTH - 1 - back
        xc = xc + cw[tap:tap + 1, :] * pltpu.roll(xp, back, 0)[SUBLANES:SUBLANES + ts, :]
    xpad_ref[0:SUBLANES, :] = xp[ts:ts + SUBLANES, :]

    xcb = [xc[:, LANES * j:LANES * (j + 1)].astype(BF16) for j in range(c // LANES)]
    parts = [jnp.dot(jnp.concatenate(xcb[ks // LANES:(ks + GATE_WIN) // LANES], axis=1), wg_ref[n],
                     preferred_element_type=F32)
             for n, ks in enumerate(_gate_windows())]
    half_bg = bg_ref[...]
    tr = jnp.tanh(jnp.concatenate([p[:, :MXU_DIM] for p in parts], axis=1) + half_bg[:, :c])
    ti = jnp.tanh(jnp.concatenate([p[:, MXU_DIM:] for p in parts], axis=1) + half_bg[:, c:])
    lam = lam_ref[...]
    log_sig_lam = jnp.minimum(lam, 0.0) - jnp.log(1.0 + jnp.exp(-jnp.abs(lam)))
    half_c = (0.5 * LRU_C) * log_sig_lam
    log_a = half_c * tr + half_c
    a = jnp.exp(log_a)
    m2 = jnp.tanh(-log_a) * (1.0 + a * a)
    mult = jnp.where(m2 > 0.0, m2 * lax.rsqrt(m2), 0.0)
    half_xc = 0.5 * xc
    b = mult * (half_xc * ti + half_xc)
    silu_gate = half_gate * jnp.tanh(half_gate) + half_gate

    row = lax.broadcasted_iota(jnp.int32, (SUBLANES, c), 0)
    hprev = hprev_ref[...]
    groups_per_chunk = ts // (SUBLANES * LRU_OUT_CHUNKS)
    for chunk in range(LRU_OUT_CHUNKS):
        hs = []
        for g in range(chunk * groups_per_chunk, (chunk + 1) * groups_per_chunk):
            av = a[SUBLANES * g:SUBLANES * (g + 1), :]
            bv = b[SUBLANES * g:SUBLANES * (g + 1), :]
            for k in (1, 2, 4):
                a_sh = jnp.where(row >= k, pltpu.roll(av, k, 0), 1.0)
                b_sh = jnp.where(row >= k, pltpu.roll(bv, k, 0), 0.0)
                bv = av * b_sh + bv
                av = av * a_sh
            hv = av * hprev + bv
            hs.append(hv)
            hprev = jnp.broadcast_to(hv[SUBLANES - 1:SUBLANES, :], (SUBLANES, c))
        rows = slice(chunk * ts // LRU_OUT_CHUNKS, (chunk + 1) * ts // LRU_OUT_CHUNKS)
        y = (jnp.concatenate(hs, axis=0) * silu_gate[rows, :]).astype(BF16)
        o_ref[0, rows, :] = xt[rows, :] + jnp.dot(y, wout_ref[...], preferred_element_type=F32)
    hprev_ref[...] = hprev


def _lru_layer(x, g, w_in, conv_w, conv_b, w_gates, b_gates, lam, w_out):
    bsz, s, d = x.shape
    c = LRU_WIDTH
    const = lambda b, t: (0, 0)
    return pl.pallas_call(
        _lru_layer_kernel,
        grid=(bsz, s // LRU_TS),
        in_specs=[
            pl.BlockSpec((1, LRU_TS, d), lambda b, t: (b, t, 0)),
            pl.BlockSpec((1, d), const),
            pl.BlockSpec((d, 2 * c), const),
            pl.BlockSpec((CONV_WIDTH, c), const),
            pl.BlockSpec((1, c), const),
            pl.BlockSpec((c // MXU_DIM, GATE_WIN, 2 * MXU_DIM), lambda b, t: (0, 0, 0)),
            pl.BlockSpec((1, 2 * c), const),
            pl.BlockSpec((1, c), const),
            pl.BlockSpec((c, d), const),
        ],
        out_specs=pl.BlockSpec((1, LRU_TS, d), lambda b, t: (b, t, 0)),
        out_shape=jax.ShapeDtypeStruct((bsz, s, d), F32),
        scratch_shapes=[
            pltpu.VMEM((LRU_TS + SUBLANES, c), F32),
            pltpu.VMEM((SUBLANES, c), F32),
        ],
        compiler_params=pltpu.CompilerParams(
            dimension_semantics=("arbitrary", "arbitrary"), vmem_limit_bytes=VMEM_LIMIT),
        name="lru_layer",
    )(x, g, w_in, conv_w, conv_b, w_gates, b_gates, lam, w_out)


def _sb_inproj_kernel(x_ref, g_ref, wk_ref, w3t_ref, k_ref, qt_ref, vt_ref, gt_ref):
    h = _rms_norm(x_ref[0], g_ref[...]).astype(BF16)
    k_ref[0] = jnp.dot(h, wk_ref[...], preferred_element_type=F32).astype(BF16)
    r = lax.dot_general(w3t_ref[...], h, (((1,), (1,)), ((), ())), preferred_element_type=F32)
    w = SB_WIDTH
    qt_ref[0, 0] = (r[0:w] * (SB_HEAD_DIM ** -0.5 * LOG2E)).astype(BF16)
    vt_ref[0, 0] = r[w:2 * w].astype(BF16)
    gt_ref[0, 0] = r[2 * w:3 * w]


def _sb_inproj(x1, g, wk, w3t):
    bsz, s, d = x1.shape
    nb = s // SB_T
    w = SB_WIDTH
    const = lambda b, t: (0, 0)
    tspec = pl.BlockSpec((1, 1, w, SB_T), lambda b, t: (b, t, 0, 0))
    return pl.pallas_call(
        _sb_inproj_kernel,
        grid=(bsz, nb),
        in_specs=[
            pl.BlockSpec((1, SB_T, d), lambda b, t: (b, t, 0)),
            pl.BlockSpec((1, d), const),
            pl.BlockSpec((d, w), const),
            pl.BlockSpec((3 * w, d), const),
        ],
        out_specs=[pl.BlockSpec((1, SB_T, w), lambda b, t: (b, t, 0)), tspec, tspec, tspec],
        out_shape=[
            jax.ShapeDtypeStruct((bsz, s, w), BF16),
            jax.ShapeDtypeStruct((bsz, nb, w, SB_T), BF16),
            jax.ShapeDtypeStruct((bsz, nb, w, SB_T), BF16),
            jax.ShapeDtypeStruct((bsz, nb, w, SB_T), F32),
        ],
        compiler_params=pltpu.CompilerParams(
            dimension_semantics=("arbitrary", "arbitrary"), vmem_limit_bytes=VMEM_LIMIT),
        name="sb_inproj",
    )(x1, g, wk, w3t)


def _softplus2(z):
    return jnp.maximum(z, 0.0) + jnp.log2(1.0 + jnp.exp2(-jnp.abs(z)))


def _sb_step(k_cur, k_next, v_prev, qpads, neg_ut, z_cur, z_next, p_scr, acc_scr, carries, mask):
    t = SB_T
    hk = SB_HALF
    n = len(qpads)
    last_row = lax.broadcasted_iota(jnp.int32, (BF16_ROWS, t), 0) == BF16_ROWS - 1

    def half_block(h, lo, carry):
        z = z_cur[h, lo:lo + hk, :]
        zm = z if mask is None else jnp.where(mask[lo:lo + hk, :], z, MASKED)
        sp = _softplus2(zm)
        if carry is not None:
            sp = jnp.concatenate([sp[:hk - BF16_ROWS], sp[hk - BF16_ROWS:] + carry], axis=0)
        lhs = jnp.concatenate([neg_ut, k_cur[h][lo:lo + hk, :]], axis=1)
        rhs = jnp.concatenate([sp.astype(BF16), qpads[h]], axis=0)
        w = jnp.dot(lhs, rhs, preferred_element_type=F32)
        total = z[0:1, :] - w[0:1, :]
        return w, jnp.where(last_row, jnp.broadcast_to(total, (BF16_ROWS, t)), 0.0)

    w_late, mid = [], []
    for h in range(n):
        w, c = half_block(h, hk, None if mask is not None else carries[h])
        w_late.append(w)
        mid.append(c)
        z_next[h] = jnp.dot(k_next[h], qpads[h], preferred_element_type=F32)
        if v_prev is not None:
            acc_scr[h] += jnp.dot(v_prev[h], p_scr[h], preferred_element_type=F32)
    carries_new = []
    for h in range(n):
        w_early, c = half_block(h, 0, mid[h])
        carries_new.append(c)
        w = jnp.concatenate([w_early, w_late[h]], axis=0)
        if mask is not None:
            w = jnp.where(mask, w, MASKED)
        p_scr[h] = jnp.exp2(w).astype(BF16)
    return tuple(carries_new)


def _sb_attn_kernel(k_ref, qt_ref, vt_ref, gt_ref, nut_ref, yt_ref, za_scr, zb_scr, p_scr, acc_scr):
    qi = pl.program_id(2)
    t = SB_T
    dh = SB_HEAD_DIM
    n = 2 * SB_PAIRS
    neg_ut = nut_ref[...]
    rows = lax.broadcasted_iota(jnp.int32, (HEAD_PAIR, t), 0)
    qpads = []
    for hd in range(n):
        qblk = qt_ref[0, 0, HEAD_PAIR * (hd // 2):HEAD_PAIR * (hd // 2 + 1), :]
        lo_row = dh * (hd % 2)
        qpads.append(jnp.where((rows >= lo_row) & (rows < lo_row + dh), qblk, jnp.zeros_like(qblk)))
    causal = (lax.broadcasted_iota(jnp.int32, (t, t), 0) < lax.broadcasted_iota(jnp.int32, (t, t), 1))

    def kblks(kj):
        koff = pl.multiple_of(jnp.maximum(kj, 0) * t, t)
        return [k_ref[0, pl.ds(koff, t), HEAD_PAIR * (hd // 2):HEAD_PAIR * (hd // 2 + 1)] for hd in range(n)]

    def vblks(kj):
        return [vt_ref[0, kj, dh * hd:dh * (hd + 1), :] for hd in range(n)]

    def step(cur, z_cur, z_next, carries, mask=None):
        v_prev = None if mask is not None else vblks(cur + 1)
        return _sb_step(kblks(cur), kblks(cur - 1), v_prev, qpads, neg_ut, z_cur, z_next, p_scr, acc_scr,
                        carries, mask)

    for hd, kb in enumerate(kblks(qi)):
        zb_scr[hd] = jnp.dot(kb, qpads[hd], preferred_element_type=F32)
        acc_scr[hd] = jnp.zeros((dh, t), F32)
    carries = step(qi, zb_scr, za_scr, None, causal)

    def pair(i, carries):
        cur = qi - 1 - 2 * i
        carries = step(cur, za_scr, zb_scr, carries)
        return step(cur - 1, zb_scr, za_scr, carries)

    carries = lax.fori_loop(0, qi // 2, pair, carries)

    @pl.when(qi % 2 == 1)
    def _():
        step(0, za_scr, zb_scr, carries)

    v_last = vblks(0)
    for hd in range(n):
        acc = acc_scr[hd] + jnp.dot(v_last[hd], p_scr[hd], preferred_element_type=F32)
        g = gt_ref[0, 0, dh * hd:dh * (hd + 1), :]
        yt_ref[0, 0, dh * hd:dh * (hd + 1), :] = (acc * (g * jax.nn.sigmoid(g))).astype(BF16)


def _sb_attn(k, qt, vt, gt, neg_ut):
    bsz, s, w = k.shape
    nb = s // SB_T
    gw = HEAD_PAIR * SB_PAIRS
    blk = pl.BlockSpec((1, 1, gw, SB_T), lambda b, j, i: (b, i, j, 0))
    return pl.pallas_call(
        _sb_attn_kernel,
        grid=(bsz, w // gw, nb),
        in_specs=[
            pl.BlockSpec((1, s, gw), lambda b, j, i: (b, 0, j)),
            blk,
            pl.BlockSpec((1, nb, gw, SB_T), lambda b, j, i: (b, 0, j, 0)),
            blk,
            pl.BlockSpec((SB_HALF, SB_HALF), lambda b, j, i: (0, 0)),
        ],
        out_specs=blk,
        out_shape=jax.ShapeDtypeStruct((bsz, nb, w, SB_T), BF16),
        scratch_shapes=[
            pltpu.VMEM((2 * SB_PAIRS, SB_T, SB_T), F32),
            pltpu.VMEM((2 * SB_PAIRS, SB_T, SB_T), F32),
            pltpu.VMEM((2 * SB_PAIRS, SB_T, SB_T), BF16),
            pltpu.VMEM((2 * SB_PAIRS, SB_HEAD_DIM, SB_T), F32),
        ],
        compiler_params=pltpu.CompilerParams(
            dimension_semantics=("arbitrary", "arbitrary", "arbitrary"), vmem_limit_bytes=VMEM_LIMIT),
        name="sb_attn",
    )(k, qt, vt, gt, neg_ut)


def _sb_out_kernel(x_ref, yt_ref, wout_ref, g_ref, o_ref):
    for blk in range(SB_OUT_BLOCKS):
        rows = slice(SB_T * blk, SB_T * (blk + 1))
        y = lax.dot_general(yt_ref[0, blk], wout_ref[...], (((0,), (0,)), ((), ())), preferred_element_type=F32)
        o_ref[0, rows, :] = _rms_norm(x_ref[0, rows, :] + y, g_ref[...])


def _sb_out(x1, yt, w_out, g):
    bsz, s, d = x1.shape
    w = SB_WIDTH
    const = lambda b, t: (0, 0)
    xspec = pl.BlockSpec((1, SB_OUT_BLOCKS * SB_T, d), lambda b, t: (b, t, 0))
    return pl.pallas_call(
        _sb_out_kernel,
        grid=(bsz, s // (SB_OUT_BLOCKS * SB_T)),
        in_specs=[
            xspec,
            pl.BlockSpec((1, SB_OUT_BLOCKS, w, SB_T), lambda b, t: (b, t, 0, 0)),
            pl.BlockSpec((w, d), const),
            pl.BlockSpec((1, d), const),
        ],
        out_specs=xspec,
        out_shape=jax.ShapeDtypeStruct((bsz, s, d), F32),
        compiler_params=pltpu.CompilerParams(
            dimension_semantics=("arbitrary", "arbitrary"), vmem_limit_bytes=VMEM_LIMIT),
        name="sb_out",
    )(x1, yt, w_out, g)


def _gate_windows():
    starts = []
    for n in range(LRU_WIDTH // MXU_DIM):
        first_blk = (MXU_DIM * n) // LRU_BLOCK
        last_blk = (MXU_DIM * (n + 1) - 1) // LRU_BLOCK
        lo, hi = LRU_BLOCK * first_blk, LRU_BLOCK * (last_blk + 1)
        ks = min(lo // LANES * LANES, LRU_WIDTH - GATE_WIN)
        assert ks <= lo and hi <= ks + GATE_WIN
        starts.append(ks)
    return starts


def _expand_gate_weights(w_gates, b_gates):
    chan = jnp.arange(LRU_WIDTH)
    repeat = (chan[None, :] % LRU_BLOCK == jnp.arange(LRU_BLOCK)[:, None]).astype(w_gates.dtype)
    same_head = (chan[:, None] // LRU_BLOCK == chan[None, :] // LRU_BLOCK).astype(w_gates.dtype)
    halves, biases = [], []
    for lo in (0, LRU_BLOCK):
        blk = w_gates[:, :, lo:lo + LRU_BLOCK].reshape(LRU_WIDTH, LRU_BLOCK)
        halves.append(jnp.dot(blk, repeat, precision=lax.Precision.HIGHEST) * same_head)
        biases.append(b_gates[:, lo:lo + LRU_BLOCK].reshape(1, LRU_WIDTH))
    tiles = [jnp.concatenate([h[ks:ks + GATE_WIN, MXU_DIM * n:MXU_DIM * (n + 1)] for h in halves], axis=1)
             for n, ks in enumerate(_gate_windows())]
    return jnp.stack(tiles), jnp.concatenate(biases, axis=1)


def kernel(x, norm_g, final_norm_g, lru_w_in, lru_conv_w, lru_conv_b, lru_w_gates, lru_b_gates, lru_lambda,
           lru_w_out, sb_w_in, sb_w_out):
    w = SB_WIDTH
    wg, bg = _expand_gate_weights(lru_w_gates[0], lru_b_gates[0])
    col_scale = jnp.concatenate([jnp.ones((1, LRU_WIDTH), F32), jnp.full((1, LRU_WIDTH), 0.5, F32)], axis=1)
    x1 = _lru_layer(
        x, norm_g[0].reshape(1, D_MODEL), (lru_w_in[0] * col_scale).astype(BF16), lru_conv_w[0],
        lru_conv_b[0].reshape(1, LRU_WIDTH), (0.5 * wg).astype(BF16), 0.5 * bg,
        lru_lambda[0].reshape(1, LRU_WIDTH), lru_w_out[0].astype(BF16))

    w_in = sb_w_in[0].astype(BF16)
    wk = w_in[:, w:2 * w]
    w3t = jnp.concatenate([w_in[:, 0:w], w_in[:, 2 * w:3 * w], w_in[:, 3 * w:4 * w]], axis=1).T
    k, qt, vt, gt = _sb_inproj(x1, norm_g[1].reshape(1, D_MODEL), wk, w3t)

    idx = jnp.arange(SB_HALF)
    neg_ut = -(idx[None, :] >= idx[:, None]).astype(BF16)
    yt = _sb_attn(k, qt, vt, gt, neg_ut)
    return _sb_out(x1, yt, sb_w_out[0].astype(BF16), final_norm_g.reshape(1, D_MODEL))
```

```python
import functools

import jax
import jax.numpy as jnp
from jax import lax
from jax.experimental import pallas as pl
from jax.experimental.pallas import tpu as pltpu

F32 = jnp.float32
BF16 = jnp.bfloat16

D_MODEL = 1024
LRU_HEADS = 16
LRU_BLOCK = 80
LRU_WIDTH = LRU_HEADS * LRU_BLOCK
LRU_C = 8.0
CONV_WIDTH = 4
SB_HEAD_DIM = 64
SB_HEADS = 16
SB_WIDTH = SB_HEADS * SB_HEAD_DIM
RMS_EPS = 1e-6

SUBLANES = 8
LANES = 128
BF16_ROWS = 16
MXU_DIM = 256
GATE_WIN = 2 * MXU_DIM
LRU_TS = 256
LRU_OUT_CHUNKS = 2
SB_T = 256
HEAD_PAIR = 2 * SB_HEAD_DIM
SB_HALF = SB_T // 2
SB_PAIRS = 8
SB_OUT_BLOCKS = 2
LOG2E = 1.4426950408889634
MASKED = -1e30
VMEM_LIMIT = 56 * 1024 * 1024


def _rms_norm(x, g):
    ms = jnp.mean(x * x, axis=-1, keepdims=True)
    return x * lax.rsqrt(ms + RMS_EPS) * g


def _lru_sb_kernel(tiles_per_seq, x_ref, g_ref, win_ref, cw_ref, cb_ref, wg_ref, bg_ref, lam_ref, wout_ref,
                   g2_ref, wk_ref, w3t_ref, o_ref, k_ref, qt_ref, vt_ref, gt_ref, xpad_ref, hprev_ref, x1_prev_ref):
    ts = LRU_TS
    c = LRU_WIDTH
    w = SB_WIDTH
    step = pl.program_id(0)

    @pl.when(step == 0)
    def _():
        x1_prev_ref[...] = jnp.zeros((ts, D_MODEL), F32)

    @pl.when(step % tiles_per_seq == 0)
    def _():
        xpad_ref[0:SUBLANES, :] = jnp.zeros((SUBLANES, c), F32)
        hprev_ref[...] = jnp.zeros((SUBLANES, c), F32)

    h2 = _rms_norm(x1_prev_ref[...], g2_ref[...]).astype(BF16)
    xt = x_ref[...]
    h = _rms_norm(xt, g_ref[...]).astype(BF16)
    u = jnp.dot(h, win_ref[...], preferred_element_type=F32)
    half_gate = u[:, c:]
    k_ref[...] = jnp.dot(h2, wk_ref[...], preferred_element_type=F32).astype(BF16)

    xpad_ref[SUBLANES:SUBLANES + ts, :] = u[:, :c]
    xp = xpad_ref[...]
    cw = cw_ref[...]
    xc = cb_ref[...] + cw[CONV_WIDTH - 1:CONV_WIDTH, :] * xp[SUBLANES:SUBLANES + ts, :]
    for back in range(1, CONV_WIDTH):
        tap = CONV_WIDTH - 1 - back
        xc = xc + cw[tap:tap + 1, :] * pltpu.roll(xp, back, 0)[SUBLANES:SUBLANES + ts, :]
    xpad_ref[0:SUBLANES, :] = xp[ts:ts + SUBLANES, :]

    xcb = [xc[:, LANES * j:LANES * (j + 1)].astype(BF16) for j in range(c // LANES)]
    parts = [jnp.dot(jnp.concatenate(xcb[ks // LANES:(ks + GATE_WIN) // LANES], axis=1), wg_ref[n],
                     preferred_element_type=F32)
             for n, ks in enumerate(_gate_windows())]
    proj_t = lambda i: lax.dot_general(w3t_ref[w * i:w * (i + 1), :], h2, (((1,), (1,)), ((), ())),
                                       preferred_element_type=F32)
    qt_ref[0] = (proj_t(0) * (SB_HEAD_DIM ** -0.5 * LOG2E)).astype(BF16)
    vt_ref[0] = proj_t(1).astype(BF16)

    half_bg = bg_ref[...]
    tr = jnp.tanh(jnp.concatenate([p[:, :MXU_DIM] for p in parts], axis=1) + half_bg[:, :c])
    ti = jnp.tanh(jnp.concatenate([p[:, MXU_DIM:] for p in parts], axis=1) + half_bg[:, c:])
    lam = lam_ref[...]
    log_sig_lam = jnp.minimum(lam, 0.0) - jnp.log(1.0 + jnp.exp(-jnp.abs(lam)))
    half_c = (0.5 * LRU_C) * log_sig_lam
    log_a = half_c * tr + half_c
    a = jnp.exp(log_a)
    m2 = jnp.tanh(-log_a) * (1.0 + a * a)
    mult = jnp.where(m2 > 0.0, m2 * lax.rsqrt(m2), 0.0)
    half_xc = 0.5 * xc
    b = mult * (half_xc * ti + half_xc)
    silu_gate = half_gate * jnp.tanh(half_gate) + half_gate

    row = lax.broadcasted_iota(jnp.int32, (SUBLANES, c), 0)
    hprev = hprev_ref[...]
    groups_per_chunk = ts // (SUBLANES * LRU_OUT_CHUNKS)
    for chunk in range(LRU_OUT_CHUNKS):
        hs = []
        for g in range(chunk * groups_per_chunk, (chunk + 1) * groups_per_chunk):
            av = a[SUBLANES * g:SUBLANES * (g + 1), :]
            bv = b[SUBLANES * g:SUBLANES * (g + 1), :]
            for k in (1, 2, 4):
                a_sh = jnp.where(row >= k, pltpu.roll(av, k, 0), 1.0)
                b_sh = jnp.where(row >= k, pltpu.roll(bv, k, 0), 0.0)
                bv = av * b_sh + bv
                av = av * a_sh
            hv = av * hprev + bv
            hs.append(hv)
            hprev = jnp.broadcast_to(hv[SUBLANES - 1:SUBLANES, :], (SUBLANES, c))
        rows = slice(chunk * ts // LRU_OUT_CHUNKS, (chunk + 1) * ts // LRU_OUT_CHUNKS)
        y = (jnp.concatenate(hs, axis=0) * silu_gate[rows, :]).astype(BF16)
        x1 = xt[rows, :] + jnp.dot(y, wout_ref[...], preferred_element_type=F32)
        o_ref[rows, :] = x1
        x1_prev_ref[rows, :] = x1
        if chunk == 0:
            gt_ref[0] = proj_t(2)
    hprev_ref[...] = hprev


def _lru_sb(x, g, w_in, conv_w, conv_b, w_gates, b_gates, lam, w_out, g2, wk, w3t):
    bsz, s, d = x.shape
    c = LRU_WIDTH
    w = SB_WIDTH
    assert LRU_TS == SB_T
    nt = s // LRU_TS
    steps = bsz * nt
    const2 = lambda t: (0, 0)
    prev = lambda t: jnp.maximum(t - 1, 0)
    weight = lambda shape: pl.BlockSpec(shape, lambda t: (0,) * len(shape), pipeline_mode=pl.Buffered(1))
    tspec = pl.BlockSpec((1, w, SB_T), lambda t: (prev(t), 0, 0))
    x1, k, qt, vt, gt = pl.pallas_call(
        functools.partial(_lru_sb_kernel, nt),
        grid=(steps,),
        in_specs=[
            pl.BlockSpec((LRU_TS, d), lambda t: (t, 0)),
            pl.BlockSpec((1, d), const2),
            weight((d, 2 * c)),
            pl.BlockSpec((CONV_WIDTH, c), const2),
            pl.BlockSpec((1, c), const2),
            weight((c // MXU_DIM, GATE_WIN, 2 * MXU_DIM)),
            pl.BlockSpec((1, 2 * c), const2),
            pl.BlockSpec((1, c), const2),
            weight((c, d)),
            pl.BlockSpec((1, d), const2),
            weight((d, w)),
            weight((3 * w, d)),
        ],
        out_specs=[
            pl.BlockSpec((LRU_TS, d), lambda t: (t, 0)),
            pl.BlockSpec((SB_T, w), lambda t: (prev(t), 0)),
            tspec, tspec, tspec,
        ],
        out_shape=[
            jax.ShapeDtypeStruct((bsz * s, d), F32),
            jax.ShapeDtypeStruct((bsz * s, w), BF16),
            jax.ShapeDtypeStruct((steps, w, SB_T), BF16),
            jax.ShapeDtypeStruct((steps, w, SB_T), BF16),
            jax.ShapeDtypeStruct((steps, w, SB_T), F32),
        ],
        scratch_shapes=[
            pltpu.VMEM((LRU_TS + SUBLANES, c), F32),
            pltpu.VMEM((SUBLANES, c), F32),
            pltpu.VMEM((LRU_TS, d), F32),
        ],
        compiler_params=pltpu.CompilerParams(dimension_semantics=("arbitrary",), vmem_limit_bytes=VMEM_LIMIT),
        name="lru_sb",
    )(x.reshape(bsz * s, d), g, w_in, conv_w, conv_b, w_gates, b_gates, lam, w_out, g2, wk, w3t)
    return x1, k, qt, vt, gt


def _sb_inproj_last_kernel(x_ref, g_ref, wk_ref, w3t_ref, k_in, qt_in, vt_in, gt_in, k_ref, qt_ref, vt_ref, gt_ref):
    del k_in, qt_in, vt_in, gt_in
    h = _rms_norm(x_ref[...], g_ref[...]).astype(BF16)
    k_ref[...] = jnp.dot(h, wk_ref[...], preferred_element_type=F32).astype(BF16)
    r = lax.dot_general(w3t_ref[...], h, (((1,), (1,)), ((), ())), preferred_element_type=F32)
    w = SB_WIDTH
    qt_ref[0] = (r[0:w] * (SB_HEAD_DIM ** -0.5 * LOG2E)).astype(BF16)
    vt_ref[0] = r[w:2 * w].astype(BF16)
    gt_ref[0] = r[2 * w:3 * w]


def _sb_inproj_last(x1, g, wk, w3t, k, qt, vt, gt):
    rows, d = x1.shape
    w = SB_WIDTH
    last = rows // SB_T - 1
    const = lambda i: (0, 0)
    anyspec = pl.BlockSpec(memory_space=pl.ANY)
    tspec = pl.BlockSpec((1, w, SB_T), lambda i: (last, 0, 0))
    return pl.pallas_call(
        _sb_inproj_last_kernel,
        grid=(1,),
        in_specs=[
            pl.BlockSpec((SB_T, d), lambda i: (last, 0)),
            pl.BlockSpec((1, d), const),
            pl.BlockSpec((d, w), const),
            pl.BlockSpec((3 * w, d), const),
            anyspec, anyspec, anyspec, anyspec,
        ],
        out_specs=[pl.BlockSpec((SB_T, w), lambda i: (last, 0)), tspec, tspec, tspec],
        out_shape=[jax.ShapeDtypeStruct(a.shape, a.dtype) for a in (k, qt, vt, gt)],
        input_output_aliases={4: 0, 5: 1, 6: 2, 7: 3},
        compiler_params=pltpu.CompilerParams(dimension_semantics=("arbitrary",), vmem_limit_bytes=VMEM_LIMIT),
        name="sb_inproj_last",
    )(x1, g, wk, w3t, k, qt, vt, gt)


def _softplus2(z):
    return jnp.maximum(z, 0.0) + jnp.log2(1.0 + jnp.exp2(-jnp.abs(z)))


def _sb_step(k_cur, k_next, v_prev, qpads, neg_ut, z_cur, z_next, p_scr, acc_scr, carries, mask):
    t = SB_T
    hk = SB_HALF
    n = len(qpads)
    last_row = lax.broadcasted_iota(jnp.int32, (BF16_ROWS, t), 0) == BF16_ROWS - 1

    def half_block(h, lo, carry):
        z = z_cur[h, lo:lo + hk, :]
        zm = z if mask is None else jnp.where(mask[lo:lo + hk, :], z, MASKED)
        sp = _softplus2(zm)
        if carry is not None:
            sp = jnp.concatenate([sp[:hk - BF16_ROWS], sp[hk - BF16_ROWS:] + carry], axis=0)
        lhs = jnp.concatenate([neg_ut, k_cur[h][lo:lo + hk, :]], axis=1)
        rhs = jnp.concatenate([sp.astype(BF16), qpads[h]], axis=0)
        w = jnp.dot(lhs, rhs, preferred_element_type=F32)
        total = z[0:1, :] - w[0:1, :]
        return w, jnp.where(last_row, jnp.broadcast_to(total, (BF16_ROWS, t)), 0.0)

    w_late, mid = [], []
    for h in range(n):
        w, c = half_block(h, hk, None if mask is not None else carries[h])
        w_late.append(w)
        mid.append(c)
        z_next[h] = jnp.dot(k_next[h], qpads[h], preferred_element_type=F32)
        if v_prev is not None:
            acc_scr[h] += jnp.dot(v_prev[h], p_scr[h], preferred_element_type=F32)
    carries_new = []
    for h in range(n):
        w_early, c = half_block(h, 0, mid[h])
        carries_new.append(c)
        w = jnp.concatenate([w_early, w_late[h]], axis=0)
        if mask is not None:
            w = jnp.where(mask, w, MASKED)
        p_scr[h] = jnp.exp2(w).astype(BF16)
    return tuple(carries_new)


def _sb_attn_kernel(k_ref, qt_ref, vt_ref, gt_ref, nut_ref, yt_ref, za_scr, zb_scr, p_scr, acc_scr):
    qi = pl.program_id(2)
    t = SB_T
    dh = SB_HEAD_DIM
    n = 2 * SB_PAIRS
    neg_ut = nut_ref[...]
    rows = lax.broadcasted_iota(jnp.int32, (HEAD_PAIR, t), 0)
    qpads = []
    for hd in range(n):
        qblk = qt_ref[0, 0, HEAD_PAIR * (hd // 2):HEAD_PAIR * (hd // 2 + 1), :]
        lo_row = dh * (hd % 2)
        qpads.append(jnp.where((rows >= lo_row) & (rows < lo_row + dh), qblk, jnp.zeros_like(qblk)))
    causal = (lax.broadcasted_iota(jnp.int32, (t, t), 0) < lax.broadcasted_iota(jnp.int32, (t, t), 1))

    def kblks(kj):
        koff = pl.multiple_of(jnp.maximum(kj, 0) * t, t)
        return [k_ref[0, pl.ds(koff, t), HEAD_PAIR * (hd // 2):HEAD_PAIR * (hd // 2 + 1)] for hd in range(n)]

    def vblks(kj):
        return [vt_ref[0, kj, dh * hd:dh * (hd + 1), :] for hd in range(n)]

    def step(cur, z_cur, z_next, carries, mask=None):
        v_prev = None if mask is not None else vblks(cur + 1)
        return _sb_step(kblks(cur), kblks(cur - 1), v_prev, qpads, neg_ut, z_cur, z_next, p_scr, acc_scr,
                        carries, mask)

    for hd, kb in enumerate(kblks(qi)):
        zb_scr[hd] = jnp.dot(kb, qpads[hd], preferred_element_type=F32)
        acc_scr[hd] = jnp.zeros((dh, t), F32)
    carries = step(qi, zb_scr, za_scr, None, causal)

    def pair(i, carries):
        cur = qi - 1 - 2 * i
        carries = step(cur, za_scr, zb_scr, carries)
        return step(cur - 1, zb_scr, za_scr, carries)

    carries = lax.fori_loop(0, qi // 2, pair, carries)

    @pl.when(qi % 2 == 1)
    def _():
        step(0, za_scr, zb_scr, carries)

    v_last = vblks(0)
    for hd in range(n):
        acc = acc_scr[hd] + jnp.dot(v_last[hd], p_scr[hd], preferred_element_type=F32)
        g = gt_ref[0, 0, dh * hd:dh * (hd + 1), :]
        yt_ref[0, 0, dh * hd:dh * (hd + 1), :] = (acc * (g * jax.nn.sigmoid(g))).astype(BF16)


def _sb_attn(k, qt, vt, gt, neg_ut):
    bsz, s, w = k.shape
    nb = s // SB_T
    gw = HEAD_PAIR * SB_PAIRS
    blk = pl.BlockSpec((1, 1, gw, SB_T), lambda b, j, i: (b, i, j, 0))
    return pl.pallas_call(
        _sb_attn_kernel,
        grid=(bsz, w // gw, nb),
        in_specs=[
            pl.BlockSpec((1, s, gw), lambda b, j, i: (b, 0, j)),
            blk,
            pl.BlockSpec((1, nb, gw, SB_T), lambda b, j, i: (b, 0, j, 0)),
            blk,
            pl.BlockSpec((SB_HALF, SB_HALF), lambda b, j, i: (0, 0)),
        ],
        out_specs=blk,
        out_shape=jax.ShapeDtypeStruct((bsz, nb, w, SB_T), BF16),
        scratch_shapes=[
            pltpu.VMEM((2 * SB_PAIRS, SB_T, SB_T), F32),
            pltpu.VMEM((2 * SB_PAIRS, SB_T, SB_T), F32),
            pltpu.VMEM((2 * SB_PAIRS, SB_T, SB_T), BF16),
            pltpu.VMEM((2 * SB_PAIRS, SB_HEAD_DIM, SB_T), F32),
        ],
        compiler_params=pltpu.CompilerParams(
            dimension_semantics=("arbitrary", "arbitrary", "arbitrary"), vmem_limit_bytes=VMEM_LIMIT),
        name="sb_attn",
    )(k, qt, vt, gt, neg_ut)


def _sb_out_kernel(x_ref, yt_ref, wout_ref, g_ref, o_ref):
    for blk in range(SB_OUT_BLOCKS):
        rows = slice(SB_T * blk, SB_T * (blk + 1))
        y = lax.dot_general(yt_ref[0, blk], wout_ref[...], (((0,), (0,)), ((), ())), preferred_element_type=F32)
        o_ref[0, rows, :] = _rms_norm(x_ref[0, rows, :] + y, g_ref[...])


def _sb_out(x1, yt, w_out, g):
    bsz, s, d = x1.shape
    w = SB_WIDTH
    const = lambda b, t: (0, 0)
    xspec = pl.BlockSpec((1, SB_OUT_BLOCKS * SB_T, d), lambda b, t: (b, t, 0))
    return pl.pallas_call(
        _sb_out_kernel,
        grid=(bsz, s // (SB_OUT_BLOCKS * SB_T)),
        in_specs=[
            xspec,
            pl.BlockSpec((1, SB_OUT_BLOCKS, w, SB_T), lambda b, t: (b, t, 0, 0)),
            pl.BlockSpec((w, d), const),
            pl.BlockSpec((1, d), const),
        ],
        out_specs=xspec,
        out_shape=jax.ShapeDtypeStruct((bsz, s, d), F32),
        compiler_params=pltpu.CompilerParams(
            dimension_semantics=("arbitrary", "arbitrary"), vmem_limit_bytes=VMEM_LIMIT),
        name="sb_out",
    )(x1, yt, w_out, g)


def _gate_windows():
    starts = []
    for n in range(LRU_WIDTH // MXU_DIM):
        first_blk = (MXU_DIM * n) // LRU_BLOCK
        last_blk = (MXU_DIM * (n + 1) - 1) // LRU_BLOCK
        lo, hi = LRU_BLOCK * first_blk, LRU_BLOCK * (last_blk + 1)
        ks = min(lo // LANES * LANES, LRU_WIDTH - GATE_WIN)
        assert ks <= lo and hi <= ks + GATE_WIN
        starts.append(ks)
    return starts


def _expand_gate_weights(w_gates, b_gates):
    chan = jnp.arange(LRU_WIDTH)
    repeat = (chan[None, :] % LRU_BLOCK == jnp.arange(LRU_BLOCK)[:, None]).astype(w_gates.dtype)
    same_head = (chan[:, None] // LRU_BLOCK == chan[None, :] // LRU_BLOCK).astype(w_gates.dtype)
    halves, biases = [], []
    for lo in (0, LRU_BLOCK):
        blk = w_gates[:, :, lo:lo + LRU_BLOCK].reshape(LRU_WIDTH, LRU_BLOCK)
        halves.append(jnp.dot(blk, repeat, precision=lax.Precision.HIGHEST) * same_head)
        biases.append(b_gates[:, lo:lo + LRU_BLOCK].reshape(1, LRU_WIDTH))
    tiles = [jnp.concatenate([h[ks:ks + GATE_WIN, MXU_DIM * n:MXU_DIM * (n + 1)] for h in halves], axis=1)
             for n, ks in enumerate(_gate_windows())]
    return jnp.stack(tiles), jnp.concatenate(biases, axis=1)


def kernel(x, norm_g, final_norm_g, lru_w_in, lru_conv_w, lru_conv_b, lru_w_gates, lru_b_gates, lru_lambda,
           lru_w_out, sb_w_in, sb_w_out):
    w = SB_WIDTH
    wg, bg = _expand_gate_weights(lru_w_gates[0], lru_b_gates[0])
    col_scale = jnp.concatenate([jnp.ones((1, LRU_WIDTH), F32), jnp.full((1, LRU_WIDTH), 0.5, F32)], axis=1)
    w_in = sb_w_in[0].astype(BF16)
    wk = w_in[:, w:2 * w]
    w3t = jnp.concatenate([w_in[:, 0:w], w_in[:, 2 * w:3 * w], w_in[:, 3 * w:4 * w]], axis=1).T
    g2 = norm_g[1].reshape(1, D_MODEL)
    x1, k, qt, vt, gt = _lru_sb(
        x, norm_g[0].reshape(1, D_MODEL), (lru_w_in[0] * col_scale).astype(BF16), lru_conv_w[0],
        lru_conv_b[0].reshape(1, LRU_WIDTH), (0.5 * wg).astype(BF16), 0.5 * bg,
        lru_lambda[0].reshape(1, LRU_WIDTH), lru_w_out[0].astype(BF16), g2, wk, w3t)
    k, qt, vt, gt = _sb_inproj_last(x1, g2, wk, w3t, k, qt, vt, gt)
    bsz, s, _ = x.shape
    nb = s // SB_T
    x1 = x1.reshape(bsz, s, D_MODEL)
    k = k.reshape(bsz, s, w)
    qt, vt, gt = (a.reshape(bsz, nb, w, SB_T) for a in (qt, vt, gt))

    idx = jnp.arange(SB_HALF)
    neg_ut = -(idx[None, :] >= idx[:, None]).astype(BF16)
    yt = _sb_attn(k, qt, vt, gt, neg_ut)
    return _sb_out(x1, yt, sb_w_out[0].astype(BF16), final_norm_g.reshape(1, D_MODEL))
```

```python
import functools

import jax
import jax.numpy as jnp
from jax import lax
from jax.experimental import pallas as pl
from jax.experimental.pallas import tpu as pltpu

F32 = jnp.float32
BF16 = jnp.bfloat16

D_MODEL = 1024
LRU_HEADS = 16
LRU_BLOCK = 80
LRU_WIDTH = LRU_HEADS * LRU_BLOCK
LRU_C = 8.0
CONV_WIDTH = 4
SB_HEAD_DIM = 64
SB_HEADS = 16
SB_WIDTH = SB_HEADS * SB_HEAD_DIM
RMS_EPS = 1e-6

SUBLANES = 8
LANES = 128
BF16_ROWS = 16
MXU_DIM = 256
GATE_WIN = 2 * MXU_DIM
LRU_TS = 256
LRU_OUT_CHUNKS = 2
SB_T = 256
HEAD_PAIR = 2 * SB_HEAD_DIM
SB_HALF = SB_T // 2
SB_PAIRS = 8
SB_OUT_BLOCKS = 4
LOG2E = 1.4426950408889634
MASKED = -1e30
VMEM_LIMIT = 56 * 1024 * 1024


def _rms_norm(x, g):
    ms = jnp.mean(x * x, axis=-1, keepdims=True)
    return x * lax.rsqrt(ms + RMS_EPS) * g


def _lru_sb_kernel(tiles_per_seq, x_ref, g_ref, win_ref, cw_ref, cb_ref, wg_ref, bg_ref, lam_ref, wout_ref,
                   g2_ref, wk_ref, w3t_ref, o_ref, k_ref, qt_ref, vt_ref, gt_ref, xpad_ref, hprev_ref, x1_prev_ref):
    ts = LRU_TS
    c = LRU_WIDTH
    w = SB_WIDTH
    step = pl.program_id(0)

    @pl.when(step == 0)
    def _():
        x1_prev_ref[...] = jnp.zeros((ts, D_MODEL), F32)

    @pl.when(step % tiles_per_seq == 0)
    def _():
        xpad_ref[0:SUBLANES, :] = jnp.zeros((SUBLANES, c), F32)
        hprev_ref[...] = jnp.zeros((SUBLANES, c), F32)

    h2 = _rms_norm(x1_prev_ref[...], g2_ref[...]).astype(BF16)
    xt = x_ref[...]
    h = _rms_norm(xt, g_ref[...]).astype(BF16)
    branch = jnp.dot(h, win_ref[:, :c], preferred_element_type=F32)
    k_ref[...] = jnp.dot(h2, wk_ref[...], preferred_element_type=F32).astype(BF16)

    xpad_ref[SUBLANES:SUBLANES + ts, :] = branch
    xp = xpad_ref[...]
    cw = cw_ref[...]
    xc = cb_ref[...] + cw[CONV_WIDTH - 1:CONV_WIDTH, :] * xp[SUBLANES:SUBLANES + ts, :]
    for back in range(1, CONV_WIDTH):
        tap = CONV_WIDTH - 1 - back
        xc = xc + cw[tap:tap + 1, :] * pltpu.roll(xp, back, 0)[SUBLANES:SUBLANES + ts, :]
    xpad_ref[0:SUBLANES, :] = xp[ts:ts + SUBLANES, :]

    xcb = [xc[:, LANES * j:LANES * (j + 1)].astype(BF16) for j in range(c // LANES)]
    parts = [jnp.dot(jnp.concatenate(xcb[ks // LANES:(ks + GATE_WIN) // LANES], axis=1), wg_ref[n],
                     preferred_element_type=F32)
             for n, ks in enumerate(_gate_windows())]
    half_gate = jnp.dot(h, win_ref[:, c:], preferred_element_type=F32)
    proj_t = lambda i: lax.dot_general(w3t_ref[w * i:w * (i + 1), :], h2, (((1,), (1,)), ((), ())),
                                       preferred_element_type=F32)
    qt_ref[0] = (proj_t(0) * (SB_HEAD_DIM ** -0.5 * LOG2E)).astype(BF16)
    vt_ref[0] = proj_t(1).astype(BF16)

    half_bg = bg_ref[...]
    tr = jnp.tanh(jnp.concatenate([p[:, :MXU_DIM] for p in parts], axis=1) + half_bg[:, :c])
    ti = jnp.tanh(jnp.concatenate([p[:, MXU_DIM:] for p in parts], axis=1) + half_bg[:, c:])
    lam = lam_ref[...]
    log_sig_lam = jnp.minimum(lam, 0.0) - jnp.log(1.0 + jnp.exp(-jnp.abs(lam)))
    half_c = (0.5 * LRU_C) * log_sig_lam
    log_a = half_c * tr + half_c
    a = jnp.exp(log_a)
    m2 = jnp.tanh(-log_a) * (1.0 + a * a)
    mult = jnp.where(m2 > 0.0, m2 * lax.rsqrt(m2), 0.0)
    half_xc = 0.5 * xc
    b = mult * (half_xc * ti + half_xc)
    silu_gate = half_gate * jnp.tanh(half_gate) + half_gate

    row = lax.broadcasted_iota(jnp.int32, (SUBLANES, c), 0)
    hprev = hprev_ref[...]
    groups_per_chunk = ts // (SUBLANES * LRU_OUT_CHUNKS)
    for chunk in range(LRU_OUT_CHUNKS):
        hs = []
        for g in range(chunk * groups_per_chunk, (chunk + 1) * groups_per_chunk):
            av = a[SUBLANES * g:SUBLANES * (g + 1), :]
            bv = b[SUBLANES * g:SUBLANES * (g + 1), :]
            for k in (1, 2, 4):
                a_sh = jnp.where(row >= k, pltpu.roll(av, k, 0), 1.0)
                b_sh = jnp.where(row >= k, pltpu.roll(bv, k, 0), 0.0)
                bv = av * b_sh + bv
                av = av * a_sh
            hv = av * hprev + bv
            hs.append(hv)
            hprev = jnp.broadcast_to(hv[SUBLANES - 1:SUBLANES, :], (SUBLANES, c))
        rows = slice(chunk * ts // LRU_OUT_CHUNKS, (chunk + 1) * ts // LRU_OUT_CHUNKS)
        y = (jnp.concatenate(hs, axis=0) * silu_gate[rows, :]).astype(BF16)
        x1 = xt[rows, :] + jnp.dot(y, wout_ref[...], preferred_element_type=F32)
        o_ref[rows, :] = x1
        x1_prev_ref[rows, :] = x1
        if chunk == 0:
            gt_ref[0] = proj_t(2)
    hprev_ref[...] = hprev


def _lru_sb(x, g, w_in, conv_w, conv_b, w_gates, b_gates, lam, w_out, g2, wk, w3t):
    bsz, s, d = x.shape
    c = LRU_WIDTH
    w = SB_WIDTH
    assert LRU_TS == SB_T
    nt = s // LRU_TS
    steps = bsz * nt
    const2 = lambda t: (0, 0)
    prev = lambda t: jnp.maximum(t - 1, 0)
    weight = lambda shape: pl.BlockSpec(shape, lambda t: (0,) * len(shape), pipeline_mode=pl.Buffered(1))
    tspec = pl.BlockSpec((1, w, SB_T), lambda t: (prev(t), 0, 0))
    x1, k, qt, vt, gt = pl.pallas_call(
        functools.partial(_lru_sb_kernel, nt),
        grid=(steps,),
        in_specs=[
            pl.BlockSpec((LRU_TS, d), lambda t: (t, 0)),
            pl.BlockSpec((1, d), const2),
            weight((d, 2 * c)),
            pl.BlockSpec((CONV_WIDTH, c), const2),
            pl.BlockSpec((1, c), const2),
            weight((c // MXU_DIM, GATE_WIN, 2 * MXU_DIM)),
            pl.BlockSpec((1, 2 * c), const2),
            pl.BlockSpec((1, c), const2),
            weight((c, d)),
            pl.BlockSpec((1, d), const2),
            weight((d, w)),
            weight((3 * w, d)),
        ],
        out_specs=[
            pl.BlockSpec((LRU_TS, d), lambda t: (t, 0)),
            pl.BlockSpec((SB_T, w), lambda t: (prev(t), 0)),
            tspec, tspec, tspec,
        ],
        out_shape=[
            jax.ShapeDtypeStruct((bsz * s, d), F32),
            jax.ShapeDtypeStruct((bsz * s, w), BF16),
            jax.ShapeDtypeStruct((steps, w, SB_T), BF16),
            jax.ShapeDtypeStruct((steps, w, SB_T), BF16),
            jax.ShapeDtypeStruct((steps, w, SB_T), F32),
        ],
        scratch_shapes=[
            pltpu.VMEM((LRU_TS + SUBLANES, c), F32),
            pltpu.VMEM((SUBLANES, c), F32),
            pltpu.VMEM((LRU_TS, d), F32),
        ],
        compiler_params=pltpu.CompilerParams(dimension_semantics=("arbitrary",), vmem_limit_bytes=VMEM_LIMIT),
        name="lru_sb",
    )(x.reshape(bsz * s, d), g, w_in, conv_w, conv_b, w_gates, b_gates, lam, w_out, g2, wk, w3t)
    return x1, k, qt, vt, gt


def _sb_inproj_last_kernel(x_ref, g_ref, wk_ref, w3t_ref, k_in, qt_in, vt_in, gt_in, k_ref, qt_ref, vt_ref, gt_ref):
    del k_in, qt_in, vt_in, gt_in
    h = _rms_norm(x_ref[...], g_ref[...]).astype(BF16)
    k_ref[...] = jnp.dot(h, wk_ref[...], preferred_element_type=F32).astype(BF16)
    r = lax.dot_general(w3t_ref[...], h, (((1,), (1,)), ((), ())), preferred_element_type=F32)
    w = SB_WIDTH
    qt_ref[0] = (r[0:w] * (SB_HEAD_DIM ** -0.5 * LOG2E)).astype(BF16)
    vt_ref[0] = r[w:2 * w].astype(BF16)
    gt_ref[0] = r[2 * w:3 * w]


def _sb_inproj_last(x1, g, wk, w3t, k, qt, vt, gt):
    rows, d = x1.shape
    w = SB_WIDTH
    last = rows // SB_T - 1
    const = lambda i: (0, 0)
    anyspec = pl.BlockSpec(memory_space=pl.ANY)
    tspec = pl.BlockSpec((1, w, SB_T), lambda i: (last, 0, 0))
    return pl.pallas_call(
        _sb_inproj_last_kernel,
        grid=(1,),
        in_specs=[
            pl.BlockSpec((SB_T, d), lambda i: (last, 0)),
            pl.BlockSpec((1, d), const),
            pl.BlockSpec((d, w), const),
            pl.BlockSpec((3 * w, d), const),
            anyspec, anyspec, anyspec, anyspec,
        ],
        out_specs=[pl.BlockSpec((SB_T, w), lambda i: (last, 0)), tspec, tspec, tspec],
        out_shape=[jax.ShapeDtypeStruct(a.shape, a.dtype) for a in (k, qt, vt, gt)],
        input_output_aliases={4: 0, 5: 1, 6: 2, 7: 3},
        compiler_params=pltpu.CompilerParams(dimension_semantics=("arbitrary",), vmem_limit_bytes=VMEM_LIMIT),
        name="sb_inproj_last",
    )(x1, g, wk, w3t, k, qt, vt, gt)


def _softplus2(z):
    return jnp.maximum(z, 0.0) + jnp.log2(1.0 + jnp.exp2(-jnp.abs(z)))


def _sb_step(k_cur, k_next, v_prev, qpads, neg_ut, z_cur, z_next, p_scr, acc_scr, carries, mask):
    t = SB_T
    hk = SB_HALF
    n = len(qpads)
    last_row = lax.broadcasted_iota(jnp.int32, (BF16_ROWS, t), 0) == BF16_ROWS - 1

    def half_block(h, lo, carry):
        z = z_cur[h, lo:lo + hk, :]
        zm = z if mask is None else jnp.where(mask[lo:lo + hk, :], z, MASKED)
        sp = _softplus2(zm)
        if carry is not None:
            sp = jnp.concatenate([sp[:hk - BF16_ROWS], sp[hk - BF16_ROWS:] + carry], axis=0)
        lhs = jnp.concatenate([neg_ut, k_cur[h][lo:lo + hk, :]], axis=1)
        rhs = jnp.concatenate([sp.astype(BF16), qpads[h]], axis=0)
        w = jnp.dot(lhs, rhs, preferred_element_type=F32)
        total = z[0:1, :] - w[0:1, :]
        return w, jnp.where(last_row, jnp.broadcast_to(total, (BF16_ROWS, t)), 0.0)

    w_late, mid = [], []
    for h in range(n):
        w, c = half_block(h, hk, None if mask is not None else carries[h])
        w_late.append(w)
        mid.append(c)
        z_next[h] = jnp.dot(k_next[h], qpads[h], preferred_element_type=F32)
        if v_prev is not None:
            acc_scr[h] += jnp.dot(v_prev[h], p_scr[h], preferred_element_type=F32)
    carries_new = []
    for h in range(n):
        w_early, c = half_block(h, 0, mid[h])
        carries_new.append(c)
        w = jnp.concatenate([w_early, w_late[h]], axis=0)
        if mask is not None:
            w = jnp.where(mask, w, MASKED)
        p_scr[h] = jnp.exp2(w).astype(BF16)
    return tuple(carries_new)


def _sb_attn_kernel(k_ref, qt_ref, vt_ref, gt_ref, nut_ref, yt_ref, za_scr, zb_scr, p_scr, acc_scr):
    qi = pl.program_id(2)
    t = SB_T
    dh = SB_HEAD_DIM
    n = 2 * SB_PAIRS
    neg_ut = nut_ref[...]
    rows = lax.broadcasted_iota(jnp.int32, (HEAD_PAIR, t), 0)
    qpads = []
    for hd in range(n):
        qblk = qt_ref[0, 0, HEAD_PAIR * (hd // 2):HEAD_PAIR * (hd // 2 + 1), :]
        lo_row = dh * (hd % 2)
        qpads.append(jnp.where((rows >= lo_row) & (rows < lo_row + dh), qblk, jnp.zeros_like(qblk)))
    causal = (lax.broadcasted_iota(jnp.int32, (t, t), 0) < lax.broadcasted_iota(jnp.int32, (t, t), 1))

    def kblks(kj):
        koff = pl.multiple_of(jnp.maximum(kj, 0) * t, t)
        return [k_ref[0, pl.ds(koff, t), HEAD_PAIR * (hd // 2):HEAD_PAIR * (hd // 2 + 1)] for hd in range(n)]

    def vblks(kj):
        return [vt_ref[0, kj, dh * hd:dh * (hd + 1), :] for hd in range(n)]

    def step(cur, z_cur, z_next, carries, mask=None):
        v_prev = None if mask is not None else vblks(cur + 1)
        return _sb_step(kblks(cur), kblks(cur - 1), v_prev, qpads, neg_ut, z_cur, z_next, p_scr, acc_scr,
                        carries, mask)

    for hd, kb in enumerate(kblks(qi)):
        zb_scr[hd] = jnp.dot(kb, qpads[hd], preferred_element_type=F32)
        acc_scr[hd] = jnp.zeros((dh, t), F32)
    carries = step(qi, zb_scr, za_scr, None, causal)

    def pair(i, carries):
        cur = qi - 1 - 2 * i
        carries = step(cur, za_scr, zb_scr, carries)
        return step(cur - 1, zb_scr, za_scr, carries)

    carries = lax.fori_loop(0, qi // 2, pair, carries)

    @pl.when(qi % 2 == 1)
    def _():
        step(0, za_scr, zb_scr, carries)

    v_last = vblks(0)
    for hd in range(n):
        acc = acc_scr[hd] + jnp.dot(v_last[hd], p_scr[hd], preferred_element_type=F32)
        g = gt_ref[0, 0, dh * hd:dh * (hd + 1), :]
        yt_ref[0, 0, dh * hd:dh * (hd + 1), :] = (acc * (g * jax.nn.sigmoid(g))).astype(BF16)


def _sb_attn(k, qt, vt, gt, neg_ut):
    bsz, s, w = k.shape
    nb = s // SB_T
    gw = HEAD_PAIR * SB_PAIRS
    blk = pl.BlockSpec((1, 1, gw, SB_T), lambda b, j, i: (b, i, j, 0))
    return pl.pallas_call(
        _sb_attn_kernel,
        grid=(bsz, w // gw, nb),
        in_specs=[
            pl.BlockSpec((1, s, gw), lambda b, j, i: (b, 0, j)),
            blk,
            pl.BlockSpec((1, nb, gw, SB_T), lambda b, j, i: (b, 0, j, 0)),
            blk,
            pl.BlockSpec((SB_HALF, SB_HALF), lambda b, j, i: (0, 0)),
        ],
        out_specs=blk,
        out_shape=jax.ShapeDtypeStruct((bsz, nb, w, SB_T), BF16),
        scratch_shapes=[
            pltpu.VMEM((2 * SB_PAIRS, SB_T, SB_T), F32),
            pltpu.VMEM((2 * SB_PAIRS, SB_T, SB_T), F32),
            pltpu.VMEM((2 * SB_PAIRS, SB_T, SB_T), BF16),
            pltpu.VMEM((2 * SB_PAIRS, SB_HEAD_DIM, SB_T), F32),
        ],
        compiler_params=pltpu.CompilerParams(
            dimension_semantics=("arbitrary", "arbitrary", "arbitrary"), vmem_limit_bytes=VMEM_LIMIT),
        name="sb_attn",
    )(k, qt, vt, gt, neg_ut)


def _sb_out_kernel(x_ref, yt_ref, wout_ref, g_ref, o_ref):
    for blk in range(SB_OUT_BLOCKS):
        rows = slice(SB_T * blk, SB_T * (blk + 1))
        y = lax.dot_general(yt_ref[0, blk], wout_ref[...], (((0,), (0,)), ((), ())), preferred_element_type=F32)
        o_ref[0, rows, :] = _rms_norm(x_ref[0, rows, :] + y, g_ref[...])


def _sb_out(x1, yt, w_out, g):
    bsz, s, d = x1.shape
    w = SB_WIDTH
    const = lambda b, t: (0, 0)
    xspec = pl.BlockSpec((1, SB_OUT_BLOCKS * SB_T, d), lambda b, t: (b, t, 0))
    return pl.pallas_call(
        _sb_out_kernel,
        grid=(bsz, s // (SB_OUT_BLOCKS * SB_T)),
        in_specs=[
            xspec,
            pl.BlockSpec((1, SB_OUT_BLOCKS, w, SB_T), lambda b, t: (b, t, 0, 0)),
            pl.BlockSpec((w, d), const),
            pl.BlockSpec((1, d), const),
        ],
        out_specs=xspec,
        out_shape=jax.ShapeDtypeStruct((bsz, s, d), F32),
        compiler_params=pltpu.CompilerParams(
            dimension_semantics=("arbitrary", "arbitrary"), vmem_limit_bytes=VMEM_LIMIT),
        name="sb_out",
    )(x1, yt, w_out, g)


def _gate_windows():
    starts = []
    for n in range(LRU_WIDTH // MXU_DIM):
        first_blk = (MXU_DIM * n) // LRU_BLOCK
        last_blk = (MXU_DIM * (n + 1) - 1) // LRU_BLOCK
        lo, hi = LRU_BLOCK * first_blk, LRU_BLOCK * (last_blk + 1)
        ks = min(lo // LANES * LANES, LRU_WIDTH - GATE_WIN)
        assert ks <= lo and hi <= ks + GATE_WIN
        starts.append(ks)
    return starts


def _expand_gate_weights(w_gates, b_gates):
    chan = jnp.arange(LRU_WIDTH)
    repeat = (chan[None, :] % LRU_BLOCK == jnp.arange(LRU_BLOCK)[:, None]).astype(w_gates.dtype)
    same_head = (chan[:, None] // LRU_BLOCK == chan[None, :] // LRU_BLOCK).astype(w_gates.dtype)
    halves, biases = [], []
    for lo in (0, LRU_BLOCK):
        blk = w_gates[:, :, lo:lo + LRU_BLOCK].reshape(LRU_WIDTH, LRU_BLOCK)
        halves.append(jnp.dot(blk, repeat, precision=lax.Precision.HIGHEST) * same_head)
        biases.append(b_gates[:, lo:lo + LRU_BLOCK].reshape(1, LRU_WIDTH))
    tiles = [jnp.concatenate([h[ks:ks + GATE_WIN, MXU_DIM * n:MXU_DIM * (n + 1)] for h in halves], axis=1)
             for n, ks in enumerate(_gate_windows())]
    return jnp.stack(tiles), jnp.concatenate(biases, axis=1)


def kernel(x, norm_g, final_norm_g, lru_w_in, lru_conv_w, lru_conv_b, lru_w_gates, lru_b_gates, lru_lambda,
           lru_w_out, sb_w_in, sb_w_out):
    w = SB_WIDTH
    wg, bg = _expand_gate_weights(lru_w_gates[0], lru_b_gates[0])
    col_scale = jnp.concatenate([jnp.ones((1, LRU_WIDTH), F32), jnp.full((1, LRU_WIDTH), 0.5, F32)], axis=1)
    w_in = sb_w_in[0].astype(BF16)
    wk = w_in[:, w:2 * w]
    w3t = jnp.concatenate([w_in[:, 0:w], w_in[:, 2 * w:3 * w], w_in[:, 3 * w:4 * w]], axis=1).T
    g2 = norm_g[1].reshape(1, D_MODEL)
    x1, k, qt, vt, gt = _lru_sb(
        x, norm_g[0].reshape(1, D_MODEL), (lru_w_in[0] * col_scale).astype(BF16), lru_conv_w[0],
        lru_conv_b[0].reshape(1, LRU_WIDTH), (0.5 * wg).astype(BF16), 0.5 * bg,
        lru_lambda[0].reshape(1, LRU_WIDTH), lru_w_out[0].astype(BF16), g2, wk, w3t)
    k, qt, vt, gt = _sb_inproj_last(x1, g2, wk, w3t, k, qt, vt, gt)
    bsz, s, _ = x.shape
    nb = s // SB_T
    x1 = x1.reshape(bsz, s, D_MODEL)
    k = k.reshape(bsz, s, w)
    qt, vt, gt = (a.reshape(bsz, nb, w, SB_T) for a in (qt, vt, gt))

    idx = jnp.arange(SB_HALF)
    neg_ut = -(idx[None, :] >= idx[:, None]).astype(BF16)
    yt = _sb_attn(k, qt, vt, gt, neg_ut)
    return _sb_out(x1, yt, sb_w_out[0].astype(BF16), final_norm_g.reshape(1, D_MODEL))
```

```python
import functools

import jax
import jax.numpy as jnp
from jax import lax
from jax.experimental import pallas as pl
from jax.experimental.pallas import tpu as pltpu

F32 = jnp.float32
BF16 = jnp.bfloat16

D_MODEL = 1024
LRU_HEADS = 16
LRU_BLOCK = 80
LRU_WIDTH = LRU_HEADS * LRU_BLOCK
LRU_C = 8.0
CONV_WIDTH = 4
SB_HEAD_DIM = 64
SB_HEADS = 16
SB_WIDTH = SB_HEADS * SB_HEAD_DIM
RMS_EPS = 1e-6

SUBLANES = 8
LANES = 128
BF16_ROWS = 16
MXU_DIM = 256
GATE_WIN = 2 * MXU_DIM
LRU_TS = 256
LRU_OUT_CHUNKS = 2
SB_T = 256
HEAD_PAIR = 2 * SB_HEAD_DIM
SB_HALF = SB_T // 2
SB_PAIRS = 8
SB_DIAG_READY = 8
SB_OUT_BLOCKS = 4
LOG2E = 1.4426950408889634
MASKED = -1e30
VMEM_LIMIT = 56 * 1024 * 1024


def _rms_norm(x, g):
    ms = jnp.mean(x * x, axis=-1, keepdims=True)
    return x * lax.rsqrt(ms + RMS_EPS) * g


def _lru_sb_kernel(tiles_per_seq, x_ref, g_ref, win_ref, cw_ref, cb_ref, wg_ref, bg_ref, lam_ref, wout_ref,
                   g2_ref, wk_ref, w3t_ref, o_ref, k_ref, qt_ref, vt_ref, gt_ref, xpad_ref, hprev_ref, x1_prev_ref):
    ts = LRU_TS
    c = LRU_WIDTH
    w = SB_WIDTH
    step = pl.program_id(0)

    @pl.when(step == 0)
    def _():
        x1_prev_ref[...] = jnp.zeros((ts, D_MODEL), F32)

    @pl.when(step % tiles_per_seq == 0)
    def _():
        xpad_ref[0:SUBLANES, :] = jnp.zeros((SUBLANES, c), F32)
        hprev_ref[...] = jnp.zeros((SUBLANES, c), F32)

    h2 = _rms_norm(x1_prev_ref[...], g2_ref[...]).astype(BF16)
    xt = x_ref[...]
    h = _rms_norm(xt, g_ref[...]).astype(BF16)
    branch = jnp.dot(h, win_ref[:, :c], preferred_element_type=F32)
    k_ref[...] = jnp.dot(h2, wk_ref[...], preferred_element_type=F32).astype(BF16)

    xpad_ref[SUBLANES:SUBLANES + ts, :] = branch
    xp = xpad_ref[...]
    cw = cw_ref[...]
    xc = cb_ref[...] + cw[CONV_WIDTH - 1:CONV_WIDTH, :] * xp[SUBLANES:SUBLANES + ts, :]
    for back in range(1, CONV_WIDTH):
        tap = CONV_WIDTH - 1 - back
        xc = xc + cw[tap:tap + 1, :] * pltpu.roll(xp, back, 0)[SUBLANES:SUBLANES + ts, :]
    xpad_ref[0:SUBLANES, :] = xp[ts:ts + SUBLANES, :]

    xcb = [xc[:, LANES * j:LANES * (j + 1)].astype(BF16) for j in range(c // LANES)]
    parts = [jnp.dot(jnp.concatenate(xcb[ks // LANES:(ks + GATE_WIN) // LANES], axis=1), wg_ref[n],
                     preferred_element_type=F32)
             for n, ks in enumerate(_gate_windows())]
    half_gate = jnp.dot(h, win_ref[:, c:], preferred_element_type=F32)
    proj_t = lambda i: lax.dot_general(w3t_ref[w * i:w * (i + 1), :], h2, (((1,), (1,)), ((), ())),
                                       preferred_element_type=F32)
    qt_ref[0] = (proj_t(0) * (SB_HEAD_DIM ** -0.5 * LOG2E)).astype(BF16)
    vt_ref[0] = proj_t(1).astype(BF16)

    half_bg = bg_ref[...]
    tr = jnp.tanh(jnp.concatenate([p[:, :MXU_DIM] for p in parts], axis=1) + half_bg[:, :c])
    ti = jnp.tanh(jnp.concatenate([p[:, MXU_DIM:] for p in parts], axis=1) + half_bg[:, c:])
    lam = lam_ref[...]
    log_sig_lam = jnp.minimum(lam, 0.0) - jnp.log(1.0 + jnp.exp(-jnp.abs(lam)))
    half_c = (0.5 * LRU_C) * log_sig_lam
    log_a = half_c * tr + half_c
    a = jnp.exp(log_a)
    m2 = jnp.tanh(-log_a) * (1.0 + a * a)
    mult = jnp.where(m2 > 0.0, m2 * lax.rsqrt(m2), 0.0)
    half_xc = 0.5 * xc
    b = mult * (half_xc * ti + half_xc)
    silu_gate = half_gate * jnp.tanh(half_gate) + half_gate

    row = lax.broadcasted_iota(jnp.int32, (SUBLANES, c), 0)
    hprev = hprev_ref[...]
    groups_per_chunk = ts // (SUBLANES * LRU_OUT_CHUNKS)
    for chunk in range(LRU_OUT_CHUNKS):
        hs = []
        for g in range(chunk * groups_per_chunk, (chunk + 1) * groups_per_chunk):
            av = a[SUBLANES * g:SUBLANES * (g + 1), :]
            bv = b[SUBLANES * g:SUBLANES * (g + 1), :]
            for k in (1, 2, 4):
                a_sh = jnp.where(row >= k, pltpu.roll(av, k, 0), 1.0)
                b_sh = jnp.where(row >= k, pltpu.roll(bv, k, 0), 0.0)
                bv = av * b_sh + bv
                av = av * a_sh
            hv = av * hprev + bv
            hs.append(hv)
            hprev = jnp.broadcast_to(hv[SUBLANES - 1:SUBLANES, :], (SUBLANES, c))
        rows = slice(chunk * ts // LRU_OUT_CHUNKS, (chunk + 1) * ts // LRU_OUT_CHUNKS)
        y = (jnp.concatenate(hs, axis=0) * silu_gate[rows, :]).astype(BF16)
        x1 = xt[rows, :] + jnp.dot(y, wout_ref[...], preferred_element_type=F32)
        o_ref[rows, :] = x1
        x1_prev_ref[rows, :] = x1
        if chunk == 0:
            half_g = proj_t(2)
            gt_ref[0] = half_g * jnp.tanh(half_g) + half_g
    hprev_ref[...] = hprev


def _lru_sb(x, g, w_in, conv_w, conv_b, w_gates, b_gates, lam, w_out, g2, wk, w3t):
    bsz, s, d = x.shape
    c = LRU_WIDTH
    w = SB_WIDTH
    assert LRU_TS == SB_T
    nt = s // LRU_TS
    steps = bsz * nt
    const2 = lambda t: (0, 0)
    prev = lambda t: jnp.maximum(t - 1, 0)
    weight = lambda shape: pl.BlockSpec(shape, lambda t: (0,) * len(shape), pipeline_mode=pl.Buffered(1))
    tspec = pl.BlockSpec((1, w, SB_T), lambda t: (prev(t), 0, 0))
    x1, k, qt, vt, gt = pl.pallas_call(
        functools.partial(_lru_sb_kernel, nt),
        grid=(steps,),
        in_specs=[
            pl.BlockSpec((LRU_TS, d), lambda t: (t, 0)),
            pl.BlockSpec((1, d), const2),
            weight((d, 2 * c)),
            pl.BlockSpec((CONV_WIDTH, c), const2),
            pl.BlockSpec((1, c), const2),
            weight((c // MXU_DIM, GATE_WIN, 2 * MXU_DIM)),
            pl.BlockSpec((1, 2 * c), const2),
            pl.BlockSpec((1, c), const2),
            weight((c, d)),
            pl.BlockSpec((1, d), const2),
            weight((d, w)),
            weight((3 * w, d)),
        ],
        out_specs=[
            pl.BlockSpec((LRU_TS, d), lambda t: (t, 0)),
            pl.BlockSpec((SB_T, w), lambda t: (prev(t), 0)),
            tspec, tspec, tspec,
        ],
        out_shape=[
            jax.ShapeDtypeStruct((bsz * s, d), F32),
            jax.ShapeDtypeStruct((bsz * s, w), BF16),
            jax.ShapeDtypeStruct((steps, w, SB_T), BF16),
            jax.ShapeDtypeStruct((steps, w, SB_T), BF16),
            jax.ShapeDtypeStruct((steps, w, SB_T), F32),
        ],
        scratch_shapes=[
            pltpu.VMEM((LRU_TS + SUBLANES, c), F32),
            pltpu.VMEM((SUBLANES, c), F32),
            pltpu.VMEM((LRU_TS, d), F32),
        ],
        compiler_params=pltpu.CompilerParams(dimension_semantics=("arbitrary",), vmem_limit_bytes=VMEM_LIMIT),
        name="lru_sb",
    )(x.reshape(bsz * s, d), g, w_in, conv_w, conv_b, w_gates, b_gates, lam, w_out, g2, wk, w3t)
    return x1, k, qt, vt, gt


def _sb_inproj_last_kernel(x_ref, g_ref, wk_ref, w3t_ref, k_in, qt_in, vt_in, gt_in, k_ref, qt_ref, vt_ref, gt_ref):
    del k_in, qt_in, vt_in, gt_in
    h = _rms_norm(x_ref[...], g_ref[...]).astype(BF16)
    k_ref[...] = jnp.dot(h, wk_ref[...], preferred_element_type=F32).astype(BF16)
    r = lax.dot_general(w3t_ref[...], h, (((1,), (1,)), ((), ())), preferred_element_type=F32)
    w = SB_WIDTH
    qt_ref[0] = (r[0:w] * (SB_HEAD_DIM ** -0.5 * LOG2E)).astype(BF16)
    vt_ref[0] = r[w:2 * w].astype(BF16)
    half_g = r[2 * w:3 * w]
    gt_ref[0] = half_g * jnp.tanh(half_g) + half_g


def _sb_inproj_last(x1, g, wk, w3t, k, qt, vt, gt):
    rows, d = x1.shape
    w = SB_WIDTH
    last = rows // SB_T - 1
    const = lambda i: (0, 0)
    anyspec = pl.BlockSpec(memory_space=pl.ANY)
    tspec = pl.BlockSpec((1, w, SB_T), lambda i: (last, 0, 0))
    return pl.pallas_call(
        _sb_inproj_last_kernel,
        grid=(1,),
        in_specs=[
            pl.BlockSpec((SB_T, d), lambda i: (last, 0)),
            pl.BlockSpec((1, d), const),
            pl.BlockSpec((d, w), const),
            pl.BlockSpec((3 * w, d), const),
            anyspec, anyspec, anyspec, anyspec,
        ],
        out_specs=[pl.BlockSpec((SB_T, w), lambda i: (last, 0)), tspec, tspec, tspec],
        out_shape=[jax.ShapeDtypeStruct(a.shape, a.dtype) for a in (k, qt, vt, gt)],
        input_output_aliases={4: 0, 5: 1, 6: 2, 7: 3},
        compiler_params=pltpu.CompilerParams(dimension_semantics=("arbitrary",), vmem_limit_bytes=VMEM_LIMIT),
        name="sb_inproj_last",
    )(x1, g, wk, w3t, k, qt, vt, gt)


def _softplus2(z):
    return jnp.maximum(z, 0.0) + jnp.log2(1.0 + jnp.exp2(-jnp.abs(z)))


def _sb_step(k_cur, k_next, v_prev, qpads, neg_ut, z_cur, z_next, p_scr, acc_scr, carries, mask, n_ready=None):
    t = SB_T
    hk = SB_HALF
    n = len(qpads)
    n_ready = n if n_ready is None else n_ready
    last_row = lax.broadcasted_iota(jnp.int32, (BF16_ROWS, t), 0) == BF16_ROWS - 1

    def half_block(h, lo, carry):
        z = z_cur[h, lo:lo + hk, :]
        zm = z if mask is None else jnp.where(mask[lo:lo + hk, :], z, MASKED)
        sp = _softplus2(zm)
        if carry is not None:
            sp = jnp.concatenate([sp[:hk - BF16_ROWS], sp[hk - BF16_ROWS:] + carry], axis=0)
        lhs = jnp.concatenate([neg_ut, k_cur[h][lo:lo + hk, :]], axis=1)
        rhs = jnp.concatenate([sp.astype(BF16), qpads[h]], axis=0)
        w = jnp.dot(lhs, rhs, preferred_element_type=F32)
        total = z[0:1, :] - w[0:1, :]
        return w, jnp.where(last_row, jnp.broadcast_to(total, (BF16_ROWS, t)), 0.0)

    w_late, mid = [], []
    for h in range(n):
        w, c = half_block(h, hk, None if mask is not None else carries[h])
        w_late.append(w)
        mid.append(c)
        if h + n_ready < n:
            z_cur[h + n_ready] = jnp.dot(k_cur[h + n_ready], qpads[h + n_ready], preferred_element_type=F32)
        z_next[h] = jnp.dot(k_next[h], qpads[h], preferred_element_type=F32)
        if v_prev is not None:
            acc_scr[h] += jnp.dot(v_prev[h], p_scr[h], preferred_element_type=F32)
    carries_new = []
    for h in range(n):
        w_early, c = half_block(h, 0, mid[h])
        carries_new.append(c)
        w = jnp.concatenate([w_early, w_late[h]], axis=0)
        if mask is not None:
            w = jnp.where(mask, w, MASKED)
        p_scr[h] = jnp.exp2(w).astype(BF16)
    return tuple(carries_new)


def _sb_attn_kernel(k_ref, qt_ref, vt_ref, gt_ref, nut_ref, yt_ref, za_scr, zb_scr, p_scr, acc_scr):
    qi = pl.program_id(2)
    t = SB_T
    dh = SB_HEAD_DIM
    n = 2 * SB_PAIRS
    neg_ut = nut_ref[...]
    rows = lax.broadcasted_iota(jnp.int32, (HEAD_PAIR, t), 0)
    qpads = []
    for hd in range(n):
        qblk = qt_ref[0, 0, HEAD_PAIR * (hd // 2):HEAD_PAIR * (hd // 2 + 1), :]
        lo_row = dh * (hd % 2)
        qpads.append(jnp.where((rows >= lo_row) & (rows < lo_row + dh), qblk, jnp.zeros_like(qblk)))
    causal = (lax.broadcasted_iota(jnp.int32, (t, t), 0) < lax.broadcasted_iota(jnp.int32, (t, t), 1))

    def kblks(kj):
        koff = pl.multiple_of(jnp.maximum(kj, 0) * t, t)
        return [k_ref[0, pl.ds(koff, t), HEAD_PAIR * (hd // 2):HEAD_PAIR * (hd // 2 + 1)] for hd in range(n)]

    def vblks(kj):
        return [vt_ref[0, kj, dh * hd:dh * (hd + 1), :] for hd in range(n)]

    def step(cur, z_cur, z_next, carries, mask=None, n_ready=None):
        v_prev = None if mask is not None else vblks(cur + 1)
        return _sb_step(kblks(cur), kblks(cur - 1), v_prev, qpads, neg_ut, z_cur, z_next, p_scr, acc_scr,
                        carries, mask, n_ready)

    for hd, kb in enumerate(kblks(qi)):
        acc_scr[hd] = jnp.zeros((dh, t), F32)
        if hd < SB_DIAG_READY:
            zb_scr[hd] = jnp.dot(kb, qpads[hd], preferred_element_type=F32)
    carries = step(qi, zb_scr, za_scr, None, causal, SB_DIAG_READY)

    def pair(i, carries):
        cur = qi - 1 - 2 * i
        carries = step(cur, za_scr, zb_scr, carries)
        return step(cur - 1, zb_scr, za_scr, carries)

    carries = lax.fori_loop(0, qi // 2, pair, carries)

    @pl.when(qi % 2 == 1)
    def _():
        step(0, za_scr, zb_scr, carries)

    v_last = vblks(0)
    for hd in range(n):
        acc = acc_scr[hd] + jnp.dot(v_last[hd], p_scr[hd], preferred_element_type=F32)
        yt_ref[0, 0, dh * hd:dh * (hd + 1), :] = (acc * gt_ref[0, 0, dh * hd:dh * (hd + 1), :]).astype(BF16)


def _sb_attn(k, qt, vt, gt, neg_ut):
    bsz, s, w = k.shape
    nb = s // SB_T
    gw = HEAD_PAIR * SB_PAIRS
    blk = pl.BlockSpec((1, 1, gw, SB_T), lambda b, j, i: (b, i, j, 0))
    return pl.pallas_call(
        _sb_attn_kernel,
        grid=(bsz, w // gw, nb),
        in_specs=[
            pl.BlockSpec((1, s, gw), lambda b, j, i: (b, 0, j)),
            blk,
            pl.BlockSpec((1, nb, gw, SB_T), lambda b, j, i: (b, 0, j, 0)),
            blk,
            pl.BlockSpec((SB_HALF, SB_HALF), lambda b, j, i: (0, 0)),
        ],
        out_specs=blk,
        out_shape=jax.ShapeDtypeStruct((bsz, nb, w, SB_T), BF16),
        scratch_shapes=[
            pltpu.VMEM((2 * SB_PAIRS, SB_T, SB_T), F32),
            pltpu.VMEM((2 * SB_PAIRS, SB_T, SB_T), F32),
            pltpu.VMEM((2 * SB_PAIRS, SB_T, SB_T), BF16),
            pltpu.VMEM((2 * SB_PAIRS, SB_HEAD_DIM, SB_T), F32),
        ],
        compiler_params=pltpu.CompilerParams(
            dimension_semantics=("arbitrary", "arbitrary", "arbitrary"), vmem_limit_bytes=VMEM_LIMIT),
        name="sb_attn",
    )(k, qt, vt, gt, neg_ut)


def _sb_out_kernel(x_ref, yt_ref, wout_ref, g_ref, o_ref):
    for blk in range(SB_OUT_BLOCKS):
        rows = slice(SB_T * blk, SB_T * (blk + 1))
        y = lax.dot_general(yt_ref[0, blk], wout_ref[...], (((0,), (0,)), ((), ())), preferred_element_type=F32)
        o_ref[0, rows, :] = _rms_norm(x_ref[0, rows, :] + y, g_ref[...])


def _sb_out(x1, yt, w_out, g):
    bsz, s, d = x1.shape
    w = SB_WIDTH
    const = lambda b, t: (0, 0)
    xspec = pl.BlockSpec((1, SB_OUT_BLOCKS * SB_T, d), lambda b, t: (b, t, 0))
    return pl.pallas_call(
        _sb_out_kernel,
        grid=(bsz, s // (SB_OUT_BLOCKS * SB_T)),
        in_specs=[
            xspec,
            pl.BlockSpec((1, SB_OUT_BLOCKS, w, SB_T), lambda b, t: (b, t, 0, 0)),
            pl.BlockSpec((w, d), const),
            pl.BlockSpec((1, d), const),
        ],
        out_specs=xspec,
        out_shape=jax.ShapeDtypeStruct((bsz, s, d), F32),
        compiler_params=pltpu.CompilerParams(
            dimension_semantics=("arbitrary", "arbitrary"), vmem_limit_bytes=VMEM_LIMIT),
        name="sb_out",
    )(x1, yt, w_out, g)


def _gate_windows():
    starts = []
    for n in range(LRU_WIDTH // MXU_DIM):
        first_blk = (MXU_DIM * n) // LRU_BLOCK
        last_blk = (MXU_DIM * (n + 1) - 1) // LRU_BLOCK
        lo, hi = LRU_BLOCK * first_blk, LRU_BLOCK * (last_blk + 1)
        ks = min(lo // LANES * LANES, LRU_WIDTH - GATE_WIN)
        assert ks <= lo and hi <= ks + GATE_WIN
        starts.append(ks)
    return starts


def _expand_gate_weights(w_gates, b_gates):
    chan = jnp.arange(LRU_WIDTH)
    repeat = (chan[None, :] % LRU_BLOCK == jnp.arange(LRU_BLOCK)[:, None]).astype(w_gates.dtype)
    same_head = (chan[:, None] // LRU_BLOCK == chan[None, :] // LRU_BLOCK).astype(w_gates.dtype)
    halves, biases = [], []
    for lo in (0, LRU_BLOCK):
        blk = w_gates[:, :, lo:lo + LRU_BLOCK].reshape(LRU_WIDTH, LRU_BLOCK)
        halves.append(jnp.dot(blk, repeat, precision=lax.Precision.HIGHEST) * same_head)
        biases.append(b_gates[:, lo:lo + LRU_BLOCK].reshape(1, LRU_WIDTH))
    tiles = [jnp.concatenate([h[ks:ks + GATE_WIN, MXU_DIM * n:MXU_DIM * (n + 1)] for h in halves], axis=1)
             for n, ks in enumerate(_gate_windows())]
    return jnp.stack(tiles), jnp.concatenate(biases, axis=1)


def kernel(x, norm_g, final_norm_g, lru_w_in, lru_conv_w, lru_conv_b, lru_w_gates, lru_b_gates, lru_lambda,
           lru_w_out, sb_w_in, sb_w_out):
    w = SB_WIDTH
    wg, bg = _expand_gate_weights(lru_w_gates[0], lru_b_gates[0])
    col_scale = jnp.concatenate([jnp.ones((1, LRU_WIDTH), F32), jnp.full((1, LRU_WIDTH), 0.5, F32)], axis=1)
    w_in = sb_w_in[0].astype(BF16)
    wk = w_in[:, w:2 * w]
    w3t = jnp.concatenate([w_in[:, 0:w], w_in[:, 2 * w:3 * w], 0.5 * w_in[:, 3 * w:4 * w]], axis=1).T
    g2 = norm_g[1].reshape(1, D_MODEL)
    x1, k, qt, vt, gt = _lru_sb(
        x, norm_g[0].reshape(1, D_MODEL), (lru_w_in[0] * col_scale).astype(BF16), lru_conv_w[0],
        lru_conv_b[0].reshape(1, LRU_WIDTH), (0.5 * wg).astype(BF16), 0.5 * bg,
        lru_lambda[0].reshape(1, LRU_WIDTH), lru_w_out[0].astype(BF16), g2, wk, w3t)
    k, qt, vt, gt = _sb_inproj_last(x1, g2, wk, w3t, k, qt, vt, gt)
    bsz, s, _ = x.shape
    nb = s // SB_T
    x1 = x1.reshape(bsz, s, D_MODEL)
    k = k.reshape(bsz, s, w)
    qt, vt, gt = (a.reshape(bsz, nb, w, SB_T) for a in (qt, vt, gt))

    idx = jnp.arange(SB_HALF)
    neg_ut = -(idx[None, :] >= idx[:, None]).astype(BF16)
    yt = _sb_attn(k, qt, vt, gt, neg_ut)
    return _sb_out(x1, yt, sb_w_out[0].astype(BF16), final_norm_g.reshape(1, D_MODEL))
```

```python
import functools

import jax
import jax.numpy as jnp
from jax import lax
from jax.experimental import pallas as pl
from jax.experimental.pallas import tpu as pltpu

F32 = jnp.float32
BF16 = jnp.bfloat16

D_MODEL = 1024
LRU_HEADS = 16
LRU_BLOCK = 80
LRU_WIDTH = LRU_HEADS * LRU_BLOCK
LRU_C = 8.0
CONV_WIDTH = 4
SB_HEAD_DIM = 64
SB_HEADS = 16
SB_WIDTH = SB_HEADS * SB_HEAD_DIM
RMS_EPS = 1e-6

SUBLANES = 8
LANES = 128
BF16_ROWS = 16
MXU_DIM = 256
GATE_WIN = 2 * MXU_DIM
LRU_TS = 256
LRU_OUT_CHUNKS = 2
SB_T = 256
HEAD_PAIR = 2 * SB_HEAD_DIM
SB_HALF = SB_T // 2
SB_PAIRS = 8
SB_OUT_BLOCKS = 4
SB_OUT_RING = 3
LOG2E = 1.4426950408889634
MASKED = -1e30
VMEM_LIMIT = 56 * 1024 * 1024


def _rms_norm(x, g):
    ms = jnp.mean(x * x, axis=-1, keepdims=True)
    return x * lax.rsqrt(ms + RMS_EPS) * g


def _lru_sb_kernel(tiles_per_seq, x_ref, g_ref, win_ref, cw_ref, cb_ref, wg_ref, bg_ref, lam_ref, wout_ref,
                   g2_ref, wk_ref, w3t_ref, o_ref, k_ref, qt_ref, vt_ref, gt_ref, xpad_ref, hprev_ref, x1_prev_ref):
    ts = LRU_TS
    c = LRU_WIDTH
    w = SB_WIDTH
    step = pl.program_id(0)

    @pl.when(step == 0)
    def _():
        x1_prev_ref[...] = jnp.zeros((ts, D_MODEL), F32)

    @pl.when(step % tiles_per_seq == 0)
    def _():
        xpad_ref[0:SUBLANES, :] = jnp.zeros((SUBLANES, c), F32)
        hprev_ref[...] = jnp.zeros((SUBLANES, c), F32)

    h2 = _rms_norm(x1_prev_ref[...], g2_ref[...]).astype(BF16)
    xt = x_ref[...]
    h = _rms_norm(xt, g_ref[...]).astype(BF16)
    branch = jnp.dot(h, win_ref[:, :c], preferred_element_type=F32)
    k_ref[...] = jnp.dot(h2, wk_ref[...], preferred_element_type=F32).astype(BF16)

    xpad_ref[SUBLANES:SUBLANES + ts, :] = branch
    xp = xpad_ref[...]
    cw = cw_ref[...]
    xc = cb_ref[...] + cw[CONV_WIDTH - 1:CONV_WIDTH, :] * xp[SUBLANES:SUBLANES + ts, :]
    for back in range(1, CONV_WIDTH):
        tap = CONV_WIDTH - 1 - back
        xc = xc + cw[tap:tap + 1, :] * pltpu.roll(xp, back, 0)[SUBLANES:SUBLANES + ts, :]
    xpad_ref[0:SUBLANES, :] = xp[ts:ts + SUBLANES, :]

    xcb = [xc[:, LANES * j:LANES * (j + 1)].astype(BF16) for j in range(c // LANES)]
    parts = [jnp.dot(jnp.concatenate(xcb[ks // LANES:(ks + GATE_WIN) // LANES], axis=1), wg_ref[n],
                     preferred_element_type=F32)
             for n, ks in enumerate(_gate_windows())]
    half_gate = jnp.dot(h, win_ref[:, c:], preferred_element_type=F32)
    proj_t = lambda i: lax.dot_general(w3t_ref[w * i:w * (i + 1), :], h2, (((1,), (1,)), ((), ())),
                                       preferred_element_type=F32)
    qt_ref[0] = (proj_t(0) * (SB_HEAD_DIM ** -0.5 * LOG2E)).astype(BF16)
    vt_ref[0] = proj_t(1).astype(BF16)

    half_bg = bg_ref[...]
    tr = jnp.tanh(jnp.concatenate([p[:, :MXU_DIM] for p in parts], axis=1) + half_bg[:, :c])
    ti = jnp.tanh(jnp.concatenate([p[:, MXU_DIM:] for p in parts], axis=1) + half_bg[:, c:])
    lam = lam_ref[...]
    log_sig_lam = jnp.minimum(lam, 0.0) - jnp.log(1.0 + jnp.exp(-jnp.abs(lam)))
    half_c = (0.5 * LRU_C) * log_sig_lam
    log_a = half_c * tr + half_c
    a = jnp.exp(log_a)
    m2 = jnp.tanh(-log_a) * (1.0 + a * a)
    mult = jnp.where(m2 > 0.0, m2 * lax.rsqrt(m2), 0.0)
    half_xc = 0.5 * xc
    b = mult * (half_xc * ti + half_xc)
    silu_gate = half_gate * jnp.tanh(half_gate) + half_gate

    row = lax.broadcasted_iota(jnp.int32, (SUBLANES, c), 0)
    hprev = hprev_ref[...]
    groups_per_chunk = ts // (SUBLANES * LRU_OUT_CHUNKS)
    for chunk in range(LRU_OUT_CHUNKS):
        hs = []
        for g in range(chunk * groups_per_chunk, (chunk + 1) * groups_per_chunk):
            av = a[SUBLANES * g:SUBLANES * (g + 1), :]
            bv = b[SUBLANES * g:SUBLANES * (g + 1), :]
            for k in (1, 2, 4):
                a_sh = jnp.where(row >= k, pltpu.roll(av, k, 0), 1.0)
                b_sh = jnp.where(row >= k, pltpu.roll(bv, k, 0), 0.0)
                bv = av * b_sh + bv
                av = av * a_sh
            hv = av * hprev + bv
            hs.append(hv)
            hprev = jnp.broadcast_to(hv[SUBLANES - 1:SUBLANES, :], (SUBLANES, c))
        rows = slice(chunk * ts // LRU_OUT_CHUNKS, (chunk + 1) * ts // LRU_OUT_CHUNKS)
        y = (jnp.concatenate(hs, axis=0) * silu_gate[rows, :]).astype(BF16)
        x1 = xt[rows, :] + jnp.dot(y, wout_ref[...], preferred_element_type=F32)
        o_ref[rows, :] = x1
        x1_prev_ref[rows, :] = x1
        if chunk == 0:
            gt_ref[0] = proj_t(2)
    hprev_ref[...] = hprev


def _lru_sb(x, g, w_in, conv_w, conv_b, w_gates, b_gates, lam, w_out, g2, wk, w3t):
    bsz, s, d = x.shape
    c = LRU_WIDTH
    w = SB_WIDTH
    assert LRU_TS == SB_T
    nt = s // LRU_TS
    steps = bsz * nt
    const2 = lambda t: (0, 0)
    prev = lambda t: jnp.maximum(t - 1, 0)
    weight = lambda shape: pl.BlockSpec(shape, lambda t: (0,) * len(shape), pipeline_mode=pl.Buffered(1))
    tspec = pl.BlockSpec((1, w, SB_T), lambda t: (prev(t), 0, 0))
    x1, k, qt, vt, gt = pl.pallas_call(
        functools.partial(_lru_sb_kernel, nt),
        grid=(steps,),
        in_specs=[
            pl.BlockSpec((LRU_TS, d), lambda t: (t, 0)),
            pl.BlockSpec((1, d), const2),
            weight((d, 2 * c)),
            pl.BlockSpec((CONV_WIDTH, c), const2),
            pl.BlockSpec((1, c), const2),
            weight((c // MXU_DIM, GATE_WIN, 2 * MXU_DIM)),
            pl.BlockSpec((1, 2 * c), const2),
            pl.BlockSpec((1, c), const2),
            weight((c, d)),
            pl.BlockSpec((1, d), const2),
            weight((d, w)),
            weight((3 * w, d)),
        ],
        out_specs=[
            pl.BlockSpec((LRU_TS, d), lambda t: (t, 0)),
            pl.BlockSpec((SB_T, w), lambda t: (prev(t), 0)),
            tspec, tspec, tspec,
        ],
        out_shape=[
            jax.ShapeDtypeStruct((bsz * s, d), F32),
            jax.ShapeDtypeStruct((bsz * s, w), BF16),
            jax.ShapeDtypeStruct((steps, w, SB_T), BF16),
            jax.ShapeDtypeStruct((steps, w, SB_T), BF16),
            jax.ShapeDtypeStruct((steps, w, SB_T), F32),
        ],
        scratch_shapes=[
            pltpu.VMEM((LRU_TS + SUBLANES, c), F32),
            pltpu.VMEM((SUBLANES, c), F32),
            pltpu.VMEM((LRU_TS, d), F32),
        ],
        compiler_params=pltpu.CompilerParams(dimension_semantics=("arbitrary",), vmem_limit_bytes=VMEM_LIMIT),
        name="lru_sb",
    )(x.reshape(bsz * s, d), g, w_in, conv_w, conv_b, w_gates, b_gates, lam, w_out, g2, wk, w3t)
    return x1, k, qt, vt, gt


def _sb_inproj_last_kernel(x_ref, g_ref, wk_ref, w3t_ref, k_in, qt_in, vt_in, gt_in, k_ref, qt_ref, vt_ref, gt_ref):
    del k_in, qt_in, vt_in, gt_in
    h = _rms_norm(x_ref[...], g_ref[...]).astype(BF16)
    k_ref[...] = jnp.dot(h, wk_ref[...], preferred_element_type=F32).astype(BF16)
    r = lax.dot_general(w3t_ref[...], h, (((1,), (1,)), ((), ())), preferred_element_type=F32)
    w = SB_WIDTH
    qt_ref[0] = (r[0:w] * (SB_HEAD_DIM ** -0.5 * LOG2E)).astype(BF16)
    vt_ref[0] = r[w:2 * w].astype(BF16)
    gt_ref[0] = r[2 * w:3 * w]


def _sb_inproj_last(x1, g, wk, w3t, k, qt, vt, gt):
    rows, d = x1.shape
    w = SB_WIDTH
    last = rows // SB_T - 1
    const = lambda i: (0, 0)
    anyspec = pl.BlockSpec(memory_space=pl.ANY)
    tspec = pl.BlockSpec((1, w, SB_T), lambda i: (last, 0, 0))
    return pl.pallas_call(
        _sb_inproj_last_kernel,
        grid=(1,),
        in_specs=[
            pl.BlockSpec((SB_T, d), lambda i: (last, 0)),
            pl.BlockSpec((1, d), const),
            pl.BlockSpec((d, w), const),
            pl.BlockSpec((3 * w, d), const),
            anyspec, anyspec, anyspec, anyspec,
        ],
        out_specs=[pl.BlockSpec((SB_T, w), lambda i: (last, 0)), tspec, tspec, tspec],
        out_shape=[jax.ShapeDtypeStruct(a.shape, a.dtype) for a in (k, qt, vt, gt)],
        input_output_aliases={4: 0, 5: 1, 6: 2, 7: 3},
        compiler_params=pltpu.CompilerParams(dimension_semantics=("arbitrary",), vmem_limit_bytes=VMEM_LIMIT),
        name="sb_inproj_last",
    )(x1, g, wk, w3t, k, qt, vt, gt)


def _softplus2(z):
    return jnp.maximum(z, 0.0) + jnp.log2(1.0 + jnp.exp2(-jnp.abs(z)))


def _sb_step(k_cur, k_next, v_prev, qpads, neg_ut, z_cur, z_next, p_scr, acc_scr, carries, mask):
    t = SB_T
    hk = SB_HALF
    n = len(qpads)
    last_row = lax.broadcasted_iota(jnp.int32, (BF16_ROWS, t), 0) == BF16_ROWS - 1

    def half_block(h, lo, carry):
        z = z_cur[h, lo:lo + hk, :]
        zm = z if mask is None else jnp.where(mask[lo:lo + hk, :], z, MASKED)
        sp = _softplus2(zm)
        if carry is not None:
            sp = jnp.concatenate([sp[:hk - BF16_ROWS], sp[hk - BF16_ROWS:] + carry], axis=0)
        lhs = jnp.concatenate([neg_ut, k_cur[h][lo:lo + hk, :]], axis=1)
        rhs = jnp.concatenate([sp.astype(BF16), qpads[h]], axis=0)
        w = jnp.dot(lhs, rhs, preferred_element_type=F32)
        total = z[0:1, :] - w[0:1, :]
        return w, jnp.where(last_row, jnp.broadcast_to(total, (BF16_ROWS, t)), 0.0)

    w_late, mid = [], []
    for h in range(n):
        w, c = half_block(h, hk, None if mask is not None else carries[h])
        w_late.append(w)
        mid.append(c)
        z_next[h] = jnp.dot(k_next[h], qpads[h], preferred_element_type=F32)
        if v_prev is not None:
            acc_scr[h] += jnp.dot(v_prev[h], p_scr[h], preferred_element_type=F32)
    carries_new = []
    for h in range(n):
        w_early, c = half_block(h, 0, mid[h])
        carries_new.append(c)
        w = jnp.concatenate([w_early, w_late[h]], axis=0)
        if mask is not None:
            w = jnp.where(mask, w, MASKED)
        p_scr[h] = jnp.exp2(w).astype(BF16)
    return tuple(carries_new)


def _sb_attn_kernel(k_ref, qt_ref, vt_ref, gt_ref, nut_ref, yt_ref, za_scr, zb_scr, p_scr, acc_scr):
    qi = pl.program_id(2)
    t = SB_T
    dh = SB_HEAD_DIM
    n = 2 * SB_PAIRS
    neg_ut = nut_ref[...]
    rows = lax.broadcasted_iota(jnp.int32, (HEAD_PAIR, t), 0)
    qpads = []
    for hd in range(n):
        qblk = qt_ref[0, 0, HEAD_PAIR * (hd // 2):HEAD_PAIR * (hd // 2 + 1), :]
        lo_row = dh * (hd % 2)
        qpads.append(jnp.where((rows >= lo_row) & (rows < lo_row + dh), qblk, jnp.zeros_like(qblk)))
    causal = (lax.broadcasted_iota(jnp.int32, (t, t), 0) < lax.broadcasted_iota(jnp.int32, (t, t), 1))

    def kblks(kj):
        koff = pl.multiple_of(jnp.maximum(kj, 0) * t, t)
        return [k_ref[0, pl.ds(koff, t), HEAD_PAIR * (hd // 2):HEAD_PAIR * (hd // 2 + 1)] for hd in range(n)]

    def vblks(kj):
        return [vt_ref[0, kj, dh * hd:dh * (hd + 1), :] for hd in range(n)]

    def step(cur, z_cur, z_next, carries, mask=None):
        v_prev = None if mask is not None else vblks(cur + 1)
        return _sb_step(kblks(cur), kblks(cur - 1), v_prev, qpads, neg_ut, z_cur, z_next, p_scr, acc_scr,
                        carries, mask)

    for hd, kb in enumerate(kblks(qi)):
        zb_scr[hd] = jnp.dot(kb, qpads[hd], preferred_element_type=F32)
        acc_scr[hd] = jnp.zeros((dh, t), F32)
    carries = step(qi, zb_scr, za_scr, None, causal)

    def pair(i, carries):
        cur = qi - 1 - 2 * i
        carries = step(cur, za_scr, zb_scr, carries)
        return step(cur - 1, zb_scr, za_scr, carries)

    carries = lax.fori_loop(0, qi // 2, pair, carries)

    @pl.when(qi % 2 == 1)
    def _():
        step(0, za_scr, zb_scr, carries)

    v_last = vblks(0)
    for hd in range(n):
        acc = acc_scr[hd] + jnp.dot(v_last[hd], p_scr[hd], preferred_element_type=F32)
        g = gt_ref[0, 0, dh * hd:dh * (hd + 1), :]
        yt_ref[0, 0, dh * hd:dh * (hd + 1), :] = (acc * (g * jax.nn.sigmoid(g))).astype(BF16)


def _sb_attn(k, qt, vt, gt, neg_ut):
    bsz, s, w = k.shape
    nb = s // SB_T
    gw = HEAD_PAIR * SB_PAIRS
    blk = pl.BlockSpec((1, 1, gw, SB_T), lambda b, j, i: (b, i, j, 0))
    return pl.pallas_call(
        _sb_attn_kernel,
        grid=(bsz, w // gw, nb),
        in_specs=[
            pl.BlockSpec((1, s, gw), lambda b, j, i: (b, 0, j)),
            blk,
            pl.BlockSpec((1, nb, gw, SB_T), lambda b, j, i: (b, 0, j, 0)),
            blk,
            pl.BlockSpec((SB_HALF, SB_HALF), lambda b, j, i: (0, 0)),
        ],
        out_specs=blk,
        out_shape=jax.ShapeDtypeStruct((bsz, nb, w, SB_T), BF16),
        scratch_shapes=[
            pltpu.VMEM((2 * SB_PAIRS, SB_T, SB_T), F32),
            pltpu.VMEM((2 * SB_PAIRS, SB_T, SB_T), F32),
            pltpu.VMEM((2 * SB_PAIRS, SB_T, SB_T), BF16),
            pltpu.VMEM((2 * SB_PAIRS, SB_HEAD_DIM, SB_T), F32),
        ],
        compiler_params=pltpu.CompilerParams(
            dimension_semantics=("arbitrary", "arbitrary", "arbitrary"), vmem_limit_bytes=VMEM_LIMIT),
        name="sb_attn",
    )(k, qt, vt, gt, neg_ut)


def _sb_out_kernel(x_hbm, yt_hbm, wout_ref, g_ref, o_ref, xbuf, ybuf, sem):
    s = pl.program_id(0)
    n = pl.num_programs(0)
    ahead = SB_OUT_RING - 1

    def copies(step, slot):
        return (pltpu.make_async_copy(x_hbm.at[step], xbuf.at[slot], sem.at[0, slot]),
                pltpu.make_async_copy(yt_hbm.at[step], ybuf.at[slot], sem.at[1, slot]))

    @pl.when(s == 0)
    def _():
        for j in range(ahead):
            for cp in copies(j, j):
                cp.start()

    @pl.when(s + ahead < n)
    def _():
        for cp in copies(s + ahead, (s + ahead) % SB_OUT_RING):
            cp.start()

    slot = s % SB_OUT_RING
    for cp in copies(s, slot):
        cp.wait()
    for blk in range(SB_OUT_BLOCKS):
        rows = slice(SB_T * blk, SB_T * (blk + 1))
        y = lax.dot_general(ybuf[slot, blk], wout_ref[...], (((0,), (0,)), ((), ())), preferred_element_type=F32)
        o_ref[rows, :] = _rms_norm(xbuf[slot, rows, :] + y, g_ref[...])


def _sb_out(x1, yt, w_out, g):
    bsz, s, d = x1.shape
    w = SB_WIDTH
    tile = SB_OUT_BLOCKS * SB_T
    steps = bsz * s // tile
    assert steps >= SB_OUT_RING - 1
    const = lambda t: (0, 0)
    out = pl.pallas_call(
        _sb_out_kernel,
        grid=(steps,),
        in_specs=[
            pl.BlockSpec(memory_space=pl.ANY),
            pl.BlockSpec(memory_space=pl.ANY),
            pl.BlockSpec((w, d), const),
            pl.BlockSpec((1, d), const),
        ],
        out_specs=pl.BlockSpec((tile, d), lambda t: (t, 0)),
        out_shape=jax.ShapeDtypeStruct((bsz * s, d), F32),
        scratch_shapes=[
            pltpu.VMEM((SB_OUT_RING, tile, d), F32),
            pltpu.VMEM((SB_OUT_RING, SB_OUT_BLOCKS, w, SB_T), BF16),
            pltpu.SemaphoreType.DMA((2, SB_OUT_RING)),
        ],
        compiler_params=pltpu.CompilerParams(dimension_semantics=("arbitrary",), vmem_limit_bytes=VMEM_LIMIT),
        name="sb_out",
    )(x1.reshape(steps, tile, d), yt.reshape(steps, SB_OUT_BLOCKS, w, SB_T), w_out, g)
    return out.reshape(bsz, s, d)


def _gate_windows():
    starts = []
    for n in range(LRU_WIDTH // MXU_DIM):
        first_blk = (MXU_DIM * n) // LRU_BLOCK
        last_blk = (MXU_DIM * (n + 1) - 1) // LRU_BLOCK
        lo, hi = LRU_BLOCK * first_blk, LRU_BLOCK * (last_blk + 1)
        ks = min(lo // LANES * LANES, LRU_WIDTH - GATE_WIN)
        assert ks <= lo and hi <= ks + GATE_WIN
        starts.append(ks)
    return starts


def _expand_gate_weights(w_gates, b_gates):
    chan = jnp.arange(LRU_WIDTH)
    repeat = (chan[None, :] % LRU_BLOCK == jnp.arange(LRU_BLOCK)[:, None]).astype(w_gates.dtype)
    same_head = (chan[:, None] // LRU_BLOCK == chan[None, :] // LRU_BLOCK).astype(w_gates.dtype)
    halves, biases = [], []
    for lo in (0, LRU_BLOCK):
        blk = w_gates[:, :, lo:lo + LRU_BLOCK].reshape(LRU_WIDTH, LRU_BLOCK)
        halves.append(jnp.dot(blk, repeat, precision=lax.Precision.HIGHEST) * same_head)
        biases.append(b_gates[:, lo:lo + LRU_BLOCK].reshape(1, LRU_WIDTH))
    tiles = [jnp.concatenate([h[ks:ks + GATE_WIN, MXU_DIM * n:MXU_DIM * (n + 1)] for h in halves], axis=1)
             for n, ks in enumerate(_gate_windows())]
    return jnp.stack(tiles), jnp.concatenate(biases, axis=1)


def kernel(x, norm_g, final_norm_g, lru_w_in, lru_conv_w, lru_conv_b, lru_w_gates, lru_b_gates, lru_lambda,
           lru_w_out, sb_w_in, sb_w_out):
    w = SB_WIDTH
    wg, bg = _expand_gate_weights(lru_w_gates[0], lru_b_gates[0])
    col_scale = jnp.concatenate([jnp.ones((1, LRU_WIDTH), F32), jnp.full((1, LRU_WIDTH), 0.5, F32)], axis=1)
    w_in = sb_w_in[0].astype(BF16)
    wk = w_in[:, w:2 * w]
    w3t = jnp.concatenate([w_in[:, 0:w], w_in[:, 2 * w:3 * w], w_in[:, 3 * w:4 * w]], axis=1).T
    g2 = norm_g[1].reshape(1, D_MODEL)
    x1, k, qt, vt, gt = _lru_sb(
        x, norm_g[0].reshape(1, D_MODEL), (lru_w_in[0] * col_scale).astype(BF16), lru_conv_w[0],
        lru_conv_b[0].reshape(1, LRU_WIDTH), (0.5 * wg).astype(BF16), 0.5 * bg,
        lru_lambda[0].reshape(1, LRU_WIDTH), lru_w_out[0].astype(BF16), g2, wk, w3t)
    k, qt, vt, gt = _sb_inproj_last(x1, g2, wk, w3t, k, qt, vt, gt)
    bsz, s, _ = x.shape
    nb = s // SB_T
    x1 = x1.reshape(bsz, s, D_MODEL)
    k = k.reshape(bsz, s, w)
    qt, vt, gt = (a.reshape(bsz, nb, w, SB_T) for a in (qt, vt, gt))

    idx = jnp.arange(SB_HALF)
    neg_ut = -(idx[None, :] >= idx[:, None]).astype(BF16)
    yt = _sb_attn(k, qt, vt, gt, neg_ut)
    return _sb_out(x1, yt, sb_w_out[0].astype(BF16), final_norm_g.reshape(1, D_MODEL))
```
